```python
import jax, jax.numpy as jnp
from jax import lax
import numpy as np

D_MODEL = 1024
BATCH = 8
SEQ = 4096
DEPTH = 1
DEC_BATCH = 128
DEC_SEQ = 8
PAST_LEN = 16384
PAGE_SIZE = 128

MLA_HEADS = 8
NOPE_DIM = 64
ROPE_DIM = 32
V_DIM = 64
Q_LORA = 384
KV_LORA = 256
MLA_WIDTH = MLA_HEADS * V_DIM
MLA_SCALE = (NOPE_DIM + ROPE_DIM) ** -0.5
ROPE_THETA = 10000.0
Q_BLOCK = 128
CONV_DIM = D_MODEL // 2
CONV_WIDTH = 3
MEM_LEN = 256
X_HEADS = 4
X_HEAD_DIM = 128
X_WIDTH = X_HEADS * X_HEAD_DIM
X_SCALE = X_HEAD_DIM ** -0.5
N_BRANCH = 3
NORM_EPS = 1e-6
DEEPNORM_ALPHA = (2 * DEPTH) ** 0.25
DEEPNORM_BETA = (8 * DEPTH) ** -0.25
NEG_INF = -1e30
SPLIT_SIZES = (Q_LORA, KV_LORA, ROPE_DIM, MLA_WIDTH,
               CONV_DIM, CONV_DIM, CONV_DIM, CONV_DIM,
               X_WIDTH, X_WIDTH, N_BRANCH * D_MODEL)
IN_COLS = int(sum(SPLIT_SIZES))
SPLIT_POINTS = tuple(int(v) for v in np.cumsum(SPLIT_SIZES)[:-1])

kernel_name = "hybrid_mla_shortconv_memxattn_step"


def _rms_norm(x, g):
    xf = x.astype(jnp.float32)
    y = xf * lax.rsqrt(jnp.mean(xf * xf, axis=-1, keepdims=True) + NORM_EPS)
    return (y * g.astype(jnp.float32)).astype(x.dtype)


def _layer_norm(x, g, b):
    xf = x.astype(jnp.float32)
    mu = jnp.mean(xf, axis=-1, keepdims=True)
    var = jnp.mean(jnp.square(xf - mu), axis=-1, keepdims=True)
    y = (xf - mu) * lax.rsqrt(var + NORM_EPS)
    return (y * g.astype(jnp.float32) + b.astype(jnp.float32)).astype(x.dtype)


def _rope(x, pos):
    half = ROPE_DIM // 2
    inv_freq = ROPE_THETA ** (-jnp.arange(half, dtype=jnp.float32) * (2.0 / ROPE_DIM))
    ang = pos.astype(jnp.float32)[:, None] * inv_freq[None, :]
    shape = (ang.shape[0],) + (1,) * (x.ndim - 3) + (half,)
    cos = jnp.cos(ang).reshape(shape)
    sin = jnp.sin(ang).reshape(shape)
    xf = x.astype(jnp.float32)
    x1, x2 = xf[..., :half], xf[..., half:]
    return jnp.concatenate([x1 * cos - x2 * sin, x1 * sin + x2 * cos], axis=-1).astype(x.dtype)


def _project(h, pos, w_in, q_norm_g, kv_norm_g, w_uq, w_uk):
    b, s, _ = h.shape
    (c_q, c_kv, k_r, z_mla, conv_b, conv_c, conv_h, z_conv,
     q_x, z_x, gates) = jnp.split(h @ w_in, SPLIT_POINTS, axis=-1)
    q = (_rms_norm(c_q, q_norm_g) @ w_uq).reshape(b, s, MLA_HEADS, NOPE_DIM + ROPE_DIM)
    q_lat = jnp.einsum("bshn,chn->bshc", q[..., :NOPE_DIM], w_uk)
    q_rope = _rope(q[..., NOPE_DIM:], pos)
    c_kv = _rms_norm(c_kv, kv_norm_g)
    k_r = _rope(k_r, pos)
    pre = conv_c * conv_h
    q_x = q_x.reshape(b, s, X_HEADS, X_HEAD_DIM)
    return q_lat, q_rope, c_kv, k_r, z_mla, conv_b, pre, z_conv, q_x, z_x, gates


def _latent_attention(q_lat, q_rope, ckv, kr, q_pos, k_pos):
    scores = (jnp.einsum("bqhc,btc->bhqt", q_lat, ckv)
              + jnp.einsum("bqhr,btr->bhqt", q_rope, kr)).astype(jnp.float32) * MLA_SCALE
    mask = k_pos[None, :] <= q_pos[:, None]
    scores = jnp.where(mask, scores, NEG_INF)
    p = jax.nn.softmax(scores, axis=-1).astype(ckv.dtype)
    return jnp.einsum("bhqt,btc->bqhc", p, ckv)


def _mla_prompt(q_lat, q_rope, c_kv, k_r, w_uv):
    b, s = q_lat.shape[:2]
    nb = s // Q_BLOCK
    pos = jnp.arange(s, dtype=jnp.int32)

    def to_blocks(t):
        return jnp.moveaxis(t.reshape((b, nb, Q_BLOCK) + t.shape[2:]), 1, 0)

    def one_block(args):
        ql, qr, qp = args
        return _latent_attention(ql, qr, c_kv, k_r, qp, pos)

    o = lax.map(one_block, (to_blocks(q_lat), to_blocks(q_rope), pos.reshape(nb, Q_BLOCK)))
    o = jnp.moveaxis(o, 0, 1).reshape(b, s, MLA_HEADS, KV_LORA)
    return jnp.einsum("bshc,chv->bshv", o, w_uv).reshape(b, s, MLA_WIDTH)


def _mla_sample(q_lat, q_rope, c_kv_new, k_r_new, cache_ckv, cache_kr, layer, page_table, w_uv):
    b, s = q_lat.shape[:2]
    past = page_table.shape[1] * PAGE_SIZE
    k_pos = jnp.arange(past + s, dtype=jnp.int32)
    q_pos = past + jnp.arange(s, dtype=jnp.int32)

    def one_seq(args):
        pt, ql, qr, cn, kn = args
        ckv = jnp.concatenate([cache_ckv[layer, pt].reshape(past, KV_LORA), cn], axis=0)
        kr = jnp.concatenate([cache_kr[layer, pt].reshape(past, ROPE_DIM), kn], axis=0)
        return _latent_attention(ql[None], qr[None], ckv[None], kr[None], q_pos, k_pos)[0]

    o = lax.map(one_seq, (page_table, q_lat, q_rope, c_kv_new, k_r_new))
    return jnp.einsum("bshc,chv->bshv", o, w_uv).reshape(b, s, MLA_WIDTH)


def _short_conv(pre, prev, conv_w):
    s = pre.shape[1]
    full = jnp.concatenate([prev, pre], axis=1)
    out = sum(conv_w[k] * full[:, k:k + s] for k in range(CONV_WIDTH))
    return out, full[:, -(CONV_WIDTH - 1):]


def _mem_kv(mem, w_mk, w_mv):
    b, m, _ = mem.shape
    return ((mem @ w_mk).reshape(b, m, X_HEADS, X_HEAD_DIM),
            (mem @ w_mv).reshape(b, m, X_HEADS, X_HEAD_DIM))


def _cross_attention(q_x, mem_k, mem_v):
    b, s = q_x.shape[:2]
    scores = jnp.einsum("bshd,bmhd->bhsm", q_x, mem_k).astype(jnp.float32) * X_SCALE
    p = jax.nn.softmax(scores, axis=-1).astype(mem_v.dtype)
    return jnp.einsum("bhsm,bmhd->bshd", p, mem_v).reshape(b, s, X_WIDTH)


def _merge(h, y_mla, z_mla, y_conv, z_conv, y_x, z_x, gates,
           w_p_mla, w_p_conv, w_p_x, w_o, ln_g, ln_b):
    b, s, d = h.shape
    g = jax.nn.sigmoid(gates.astype(jnp.float32)).astype(h.dtype).reshape(b, s, N_BRANCH, d)
    m = (g[:, :, 0] * ((y_mla * jax.nn.silu(z_mla)) @ w_p_mla)
         + g[:, :, 1] * ((y_conv * jax.nn.silu(z_conv)) @ w_p_conv)
         + g[:, :, 2] * ((y_x * jax.nn.silu(z_x)) @ w_p_x))
    return _layer_norm(DEEPNORM_ALPHA * h + m @ w_o, ln_g, ln_b)


def setup_inputs(seed: int = 0) -> dict:
    key = jax.random.key(seed)
    ks = iter(jax.random.split(key, 32))

    def nrm(shape, scale=1.0):
        return jax.random.normal(next(ks), shape, jnp.float32) * scale

    n_pages = PAST_LEN // PAGE_SIZE
    n_used = DEC_BATCH * n_pages
    n_pool = n_used + n_used // 4
    page_table = jax.random.permutation(next(ks), n_pool)[:n_used].reshape(DEC_BATCH, n_pages).astype(jnp.int32)
    L = DEPTH
    return {
        "x_prompt": nrm((BATCH, SEQ, D_MODEL)),
        "x_sample": nrm((DEC_BATCH, DEC_SEQ, D_MODEL)),
        "mem_prompt": nrm((BATCH, MEM_LEN, D_MODEL)),
        "cache_ckv": nrm((L, n_pool, PAGE_SIZE, KV_LORA)),
        "cache_kr": nrm((L, n_pool, PAGE_SIZE, ROPE_DIM)),
        "cache_conv": nrm((L, DEC_BATCH, CONV_WIDTH - 1, CONV_DIM)),
        "cache_mem_k": nrm((L, DEC_BATCH, MEM_LEN, X_HEADS, X_HEAD_DIM)),
        "cache_mem_v": nrm((L, DEC_BATCH, MEM_LEN, X_HEADS, X_HEAD_DIM)),
        "page_table": page_table,
        "w_in": nrm((L, D_MODEL, IN_COLS), D_MODEL ** -0.5),
        "q_norm_g": 1.0 + nrm((L, Q_LORA), 0.01),
        "kv_norm_g": 1.0 + nrm((L, KV_LORA), 0.01),
        "w_uq": nrm((L, Q_LORA, MLA_HEADS * (NOPE_DIM + ROPE_DIM)), Q_LORA ** -0.5),
        "w_uk": nrm((L, KV_LORA, MLA_HEADS, NOPE_DIM), KV_LORA ** -0.5),
        "w_uv": nrm((L, KV_LORA, MLA_HEADS, V_DIM), KV_LORA ** -0.5),
        "conv_w": nrm((L, CONV_WIDTH, CONV_DIM), CONV_WIDTH ** -0.5),
        "w_mk": nrm((L, D_MODEL, X_WIDTH), D_MODEL ** -0.5),
        "w_mv": nrm((L, D_MODEL, X_WIDTH), D_MODEL ** -0.5),
        "w_p_mla": nrm((L, MLA_WIDTH, D_MODEL), MLA_WIDTH ** -0.5),
        "w_p_conv": nrm((L, CONV_DIM, D_MODEL), CONV_DIM ** -0.5),
        "w_p_x": nrm((L, X_WIDTH, D_MODEL), X_WIDTH ** -0.5),
        "w_o": nrm((L, D_MODEL, D_MODEL), DEEPNORM_BETA * D_MODEL ** -0.5),
        "ln_g": 1.0 + nrm((L, D_MODEL), 0.01),
        "ln_b": nrm((L, D_MODEL), 0.01),
    }


def reference(x_prompt, x_sample, mem_prompt, cache_ckv, cache_kr, cache_conv, cache_mem_k,
              cache_mem_v, page_table, w_in, q_norm_g, kv_norm_g, w_uq, w_uk, w_uv, conv_w,
              w_mk, w_mv, w_p_mla, w_p_conv, w_p_x, w_o, ln_g, ln_b):
    pos_p = jnp.arange(x_prompt.shape[1], dtype=jnp.float32)
    pos_s = PAST_LEN + jnp.arange(x_sample.shape[1], dtype=jnp.float32)
    hp, hs = x_prompt, x_sample
    ckv_p, kr_p, conv_p, memk_p, memv_p, ckv_s, kr_s, conv_s = [], [], [], [], [], [], [], []
    for l in range(DEPTH):
        (q_lat, q_rope, c_kv, k_r, z_mla, conv_b, pre, z_conv,
         q_x, z_x, gates) = _project(hp, pos_p, w_in[l], q_norm_g[l], kv_norm_g[l], w_uq[l], w_uk[l])
        y_mla = _mla_prompt(q_lat, q_rope, c_kv, k_r, w_uv[l])
        zero_prev = jnp.zeros((hp.shape[0], CONV_WIDTH - 1, CONV_DIM), pre.dtype)
        conv_out, conv_state = _short_conv(pre, zero_prev, conv_w[l])
        mem_k, mem_v = _mem_kv(mem_prompt, w_mk[l], w_mv[l])
        y_x = _cross_attention(q_x, mem_k, mem_v)
        hp = _merge(hp, y_mla, z_mla, conv_b * conv_out, z_conv, y_x, z_x, gates,
                    w_p_mla[l], w_p_conv[l], w_p_x[l], w_o[l], ln_g[l], ln_b[l])
        ckv_p.append(c_kv); kr_p.append(k_r); conv_p.append(conv_state)
        memk_p.append(mem_k); memv_p.append(mem_v)
        (q_lat, q_rope, c_kv, k_r, z_mla, conv_b, pre, z_conv,
         q_x, z_x, gates) = _project(hs, pos_s, w_in[l], q_norm_g[l], kv_norm_g[l], w_uq[l], w_uk[l])
        y_mla = _mla_sample(q_lat, q_rope, c_kv, k_r, cache_ckv, cache_kr, l, page_table, w_uv[l])
        conv_out, conv_state = _short_conv(pre, cache_conv[l], conv_w[l])
        y_x = _cross_attention(q_x, cache_mem_k[l], cache_mem_v[l])
        hs = _merge(hs, y_mla, z_mla, conv_b * conv_out, z_conv, y_x, z_x, gates,
                    w_p_mla[l], w_p_conv[l], w_p_x[l], w_o[l], ln_g[l], ln_b[l])
        ckv_s.append(c_kv); kr_s.append(k_r); conv_s.append(conv_state)
    return (hp, hs,
            jnp.stack(ckv_p), jnp.stack(kr_p), jnp.stack(conv_p),
            jnp.stack(memk_p), jnp.stack(memv_p),
            jnp.stack(ckv_s), jnp.stack(kr_s), jnp.stack(conv_s))
```

```python
import functools

import jax
import jax.numpy as jnp
import numpy as np
from jax import lax
from jax.experimental import pallas as pl
from jax.experimental.pallas import tpu as pltpu

D_MODEL = 1024
PAGE_SIZE = 128
MLA_HEADS = 8
NOPE_DIM = 64
ROPE_DIM = 32
V_DIM = 64
Q_LORA = 384
KV_LORA = 256
MLA_WIDTH = MLA_HEADS * V_DIM
MLA_SCALE = (NOPE_DIM + ROPE_DIM) ** -0.5
ROPE_THETA = 10000.0
CONV_DIM = D_MODEL // 2
CONV_WIDTH = 3
MEM_LEN = 256
X_HEADS = 4
X_HEAD_DIM = 128
X_WIDTH = X_HEADS * X_HEAD_DIM
X_SCALE = X_HEAD_DIM ** -0.5
N_BRANCH = 3
NORM_EPS = 1e-6
NEG_INF = -1e30
SPLIT_SIZES = (Q_LORA, KV_LORA, ROPE_DIM, MLA_WIDTH,
               CONV_DIM, CONV_DIM, CONV_DIM, CONV_DIM,
               X_WIDTH, X_WIDTH, N_BRANCH * D_MODEL)
SPLIT_POINTS = tuple(int(v) for v in np.cumsum(SPLIT_SIZES)[:-1])

ROPE_LANES = MLA_HEADS * ROPE_DIM
KCAT = KV_LORA + ROPE_LANES
VMEM_LIMIT = 48 * 1024 * 1024
BF16 = jnp.bfloat16
F32 = jnp.float32
NT_DIMS = (((1,), (1,)), ((), ()))


def _dot(a, b):
    return jnp.dot(a, b, preferred_element_type=F32)


def _dot_nt(a, b):
    return lax.dot_general(a, b, NT_DIMS, preferred_element_type=F32)


def _rms(x, g):
    return x * lax.rsqrt(jnp.mean(x * x, axis=-1, keepdims=True) + NORM_EPS) * g


def _const_spec(shape):
    return pl.BlockSpec(shape, lambda *_: (0,) * len(shape), pipeline_mode=pl.Buffered(1))


def _mod_pow2(x, n):
    assert n & (n - 1) == 0
    return x & (n - 1)


def _params(n_axes):
    return pltpu.CompilerParams(dimension_semantics=("arbitrary",) * n_axes,
                                vmem_limit_bytes=VMEM_LIMIT)


def _qkv_kernel(x_ref, wa_ref, gq_ref, gkv_ref, wuq_ref, wuk_ref, cos_ref, sin_ref,
                qlat_ref, qrope_ref, kcat_ref, ckv_ref, kr_ref):
    cos = cos_ref[...]
    sin = sin_ref[...]
    xb = x_ref[...].astype(BF16)
    pa = _dot(xb, wa_ref[...])
    cqn = _rms(pa[:, :Q_LORA], gq_ref[...])
    qa = _dot(cqn.astype(BF16), wuq_ref[...])
    r0 = MLA_HEADS * NOPE_DIM
    qrope_ref[...] = (qa[:, r0:r0 + ROPE_LANES] * cos
                      + qa[:, r0 + ROPE_LANES:] * sin).astype(qrope_ref.dtype)
    for p in range(MLA_HEADS // 2):
        ql = _dot(qa[:, 128 * p:128 * (p + 1)].astype(BF16), wuk_ref[p])
        qlat_ref[2 * p] = ql[:, :KV_LORA].astype(qlat_ref.dtype)
        qlat_ref[2 * p + 1] = ql[:, KV_LORA:].astype(qlat_ref.dtype)
    ckvn = _rms(pa[:, Q_LORA:Q_LORA + KV_LORA], gkv_ref[...])
    ckv_ref[...] = ckvn
    k0 = Q_LORA + KV_LORA
    krt = pa[:, k0:k0 + ROPE_LANES] * cos + pa[:, k0 + ROPE_LANES:] * sin
    kr_ref[...] = krt[:, :ROPE_DIM]
    kcat_ref[:, :KV_LORA] = ckvn.astype(BF16)
    kcat_ref[:, KV_LORA:] = krt.astype(BF16)


def _qkv(x2d, w, cos, sin, tm, table_tiles, q_dtype):
    ntok = x2d.shape[0]
    tok = lambda width: pl.BlockSpec((tm, width), lambda t: (t, 0))
    tab = pl.BlockSpec((tm, ROPE_LANES), lambda t: (t % table_tiles, 0))
    return pl.pallas_call(
        _qkv_kernel,
        grid=(ntok // tm,),
        in_specs=[tok(D_MODEL), _const_spec(w["wa"].shape), _const_spec((1, Q_LORA)),
                  _const_spec((1, KV_LORA)), _const_spec(w["wuq"].shape),
                  _const_spec(w["wuk"].shape), tab, tab],
        out_specs=[pl.BlockSpec((MLA_HEADS, tm, KV_LORA), lambda t: (0, t, 0)),
                   tok(ROPE_LANES), tok(KCAT), tok(KV_LORA), tok(ROPE_DIM)],
        out_shape=[jax.ShapeDtypeStruct((MLA_HEADS, ntok, KV_LORA), q_dtype),
                   jax.ShapeDtypeStruct((ntok, ROPE_LANES), F32),
                   jax.ShapeDtypeStruct((ntok, KCAT), BF16),
                   jax.ShapeDtypeStruct((ntok, KV_LORA), F32),
                   jax.ShapeDtypeStruct((ntok, ROPE_DIM), F32)],
        compiler_params=_params(1),
        name="qkv",
    )(x2d, w["wa"], w["gq"], w["gkv"], w["wuq"], w["wuk"], cos, sin)


def _softmax_update(s, v_fn, m_scr, l_scr, acc_scr):
    m_old = m_scr[...]
    m_new = jnp.maximum(m_old, jnp.max(s, axis=-1, keepdims=True))
    alpha = jnp.exp(m_old - m_new)
    p = jnp.exp(s - m_new)
    l_scr[...] = alpha * l_scr[...] + jnp.sum(p, axis=-1, keepdims=True)
    acc_scr[...] = alpha * acc_scr[...] + v_fn(p.astype(BF16))
    m_scr[...] = m_new


def _softmax_init(m_scr, l_scr, acc_scr):
    m_scr[...] = jnp.full(m_scr.shape, -jnp.inf, F32)
    l_scr[...] = jnp.zeros(l_scr.shape, F32)
    acc_scr[...] = jnp.zeros(acc_scr.shape, F32)


def _mla_prompt_kernel(qlat_ref, qrope_ref, kcat_ref, wuv_ref, y_ref,
                       q_scr, m_scr, l_scr, acc_scr, *, tq):
    i = pl.program_id(1)
    rows = MLA_HEADS * tq
    qr = qrope_ref[...]
    lane = lax.broadcasted_iota(jnp.int32, (tq, ROPE_LANES), 1)
    for h in range(MLA_HEADS):
        q_scr[h * tq:(h + 1) * tq, :KV_LORA] = qlat_ref[h]
        own = (lane >= h * ROPE_DIM) & (lane < (h + 1) * ROPE_DIM)
        q_scr[h * tq:(h + 1) * tq, KV_LORA:] = jnp.where(own, qr, 0.0).astype(BF16)
    _softmax_init(m_scr, l_scr, acc_scr)

    def step(j, diagonal):
        kt = kcat_ref[pl.ds(pl.multiple_of(j * tq, tq), tq), :]
        s = _dot_nt(q_scr[...], kt) * MLA_SCALE
        if diagonal:
            q_pos = _mod_pow2(lax.broadcasted_iota(jnp.int32, (rows, tq), 0), tq)
            k_pos = lax.broadcasted_iota(jnp.int32, (rows, tq), 1)
            s = jnp.where(k_pos <= q_pos, s, NEG_INF)
        _softmax_update(s, lambda p: _dot(p, kt[:, :KV_LORA]), m_scr, l_scr, acc_scr)

    def body(j, carry):
        step(j, False)
        return carry

    lax.fori_loop(0, i, body, 0)
    step(i, True)
    inv_l = 1.0 / l_scr[...]
    y = jnp.zeros((tq, MLA_WIDTH), F32)
    for h in range(MLA_HEADS):
        sl = slice(h * tq, (h + 1) * tq)
        y = y + _dot((acc_scr[sl, :] * inv_l[sl, :]).astype(BF16), wuv_ref[h])
    y_ref[...] = y


def _mla_prompt(qlat, qrope, kcat, wuv, batch, seq, tq):
    ntok = batch * seq
    nq = seq // tq
    rows = MLA_HEADS * tq
    return pl.pallas_call(
        functools.partial(_mla_prompt_kernel, tq=tq),
        grid=(batch, nq),
        in_specs=[pl.BlockSpec((MLA_HEADS, tq, KV_LORA), lambda b, i: (0, b * nq + i, 0)),
                  pl.BlockSpec((tq, ROPE_LANES), lambda b, i: (b * nq + i, 0)),
                  pl.BlockSpec((seq, KCAT), lambda b, i: (b, 0)),
                  _const_spec(wuv.shape)],
        out_specs=pl.BlockSpec((tq, MLA_WIDTH), lambda b, i: (b * nq + i, 0)),
        out_shape=jax.ShapeDtypeStruct((ntok, MLA_WIDTH), F32),
        scratch_shapes=[pltpu.VMEM((rows, KCAT), BF16), pltpu.VMEM((rows, 1), F32),
                        pltpu.VMEM((rows, 1), F32), pltpu.VMEM((rows, KV_LORA), F32)],
        compiler_params=_params(2),
        name="mla_prompt",
    )(qlat, qrope, kcat, wuv)


def _mla_decode_kernel(pt_ref, qlat_ref, qrope_ref, ckvn_ref, krn_ref, *rest, npg, dec_seq):
    del pt_ref
    ckv_refs = rest[:npg]
    kr_refs = rest[npg:2 * npg]
    o_ref = rest[2 * npg]
    ql_scr, qr_scr, kn_scr, krn_scr, m_scr, l_scr, acc_scr = rest[2 * npg + 1:]
    c = pl.program_id(1)
    rows = MLA_HEADS * dec_seq

    @pl.when(c == 0)
    def _():
        ql_scr[...] = jnp.concatenate([qlat_ref[h] for h in range(MLA_HEADS)], axis=0).astype(BF16)
        qr = qrope_ref[...]
        qr_scr[...] = jnp.concatenate(
            [qr[:, ROPE_DIM * h:ROPE_DIM * (h + 1)] for h in range(MLA_HEADS)], axis=0).astype(BF16)
        kn_scr[...] = jnp.zeros(kn_scr.shape, F32)
        krn_scr[...] = jnp.zeros(krn_scr.shape, F32)
        kn_scr[:dec_seq, :] = ckvn_ref[...]
        krn_scr[:dec_seq, :] = krn_ref[...]
        _softmax_init(m_scr, l_scr, acc_scr)

    ql = ql_scr[...]
    qr = qr_scr[...]

    def scores(ck, kr):
        return (_dot_nt(ql, ck) + _dot_nt(qr, kr)) * MLA_SCALE

    cks = [ckv_refs[k][0].astype(BF16) for k in range(npg)]
    s = jnp.concatenate([scores(cks[k], kr_refs[k][0].astype(BF16)) for k in range(npg)], axis=1)

    def pv(p):
        out = _dot(p[:, :PAGE_SIZE], cks[0])
        for k in range(1, npg):
            out = out + _dot(p[:, PAGE_SIZE * k:PAGE_SIZE * (k + 1)], cks[k])
        return out

    _softmax_update(s, pv, m_scr, l_scr, acc_scr)

    @pl.when(c == pl.num_programs(1) - 1)
    def _():
        ck = kn_scr[...].astype(BF16)
        s_new = scores(ck, krn_scr[...].astype(BF16))
        q_pos = _mod_pow2(lax.broadcasted_iota(jnp.int32, (rows, PAGE_SIZE), 0), dec_seq)
        k_pos = lax.broadcasted_iota(jnp.int32, (rows, PAGE_SIZE), 1)
        s_new = jnp.where(k_pos <= q_pos, s_new, NEG_INF)
        _softmax_update(s_new, lambda p: _dot(p, ck), m_scr, l_scr, acc_scr)
        o = acc_scr[...] / l_scr[...]
        for h in range(MLA_HEADS):
            o_ref[h] = o[h * dec_seq:(h + 1) * dec_seq, :]


def _mla_decode(page_table, qlat, qrope, ckv_new, kr_new, cache_ckv, cache_kr, dec_seq, npg):
    n_seq, n_pages = page_table.shape
    ntok = n_seq * dec_seq
    rows = MLA_HEADS * dec_seq

    def page_spec(width, k):
        return pl.BlockSpec((1, PAGE_SIZE, width), lambda s, c, pt: (pt[s, c * npg + k], 0, 0))

    tok = lambda width: pl.BlockSpec((dec_seq, width), lambda s, c, pt: (s, 0))
    head_tok = pl.BlockSpec((MLA_HEADS, dec_seq, KV_LORA), lambda s, c, pt: (0, s, 0))
    grid_spec = pltpu.PrefetchScalarGridSpec(
        num_scalar_prefetch=1,
        grid=(n_seq, n_pages // npg),
        in_specs=[head_tok, tok(ROPE_LANES), tok(KV_LORA), tok(ROPE_DIM)]
        + [page_spec(KV_LORA, k) for k in range(npg)]
        + [page_spec(ROPE_DIM, k) for k in range(npg)],
        out_specs=head_tok,
        scratch_shapes=[pltpu.VMEM((rows, KV_LORA), BF16), pltpu.VMEM((rows, ROPE_DIM), BF16),
                        pltpu.VMEM((PAGE_SIZE, KV_LORA), F32), pltpu.VMEM((PAGE_SIZE, ROPE_DIM), F32),
                        pltpu.VMEM((rows, 1), F32), pltpu.VMEM((rows, 1), F32),
                        pltpu.VMEM((rows, KV_LORA), F32)],
    )
    return pl.pallas_call(
        functools.partial(_mla_decode_kernel, npg=npg, dec_seq=dec_seq),
        grid_spec=grid_spec,
        out_shape=jax.ShapeDtypeStruct((MLA_HEADS, ntok, KV_LORA), F32),
        compiler_params=_params(2),
        name="mla_decode",
    )(page_table, qlat, qrope, ckv_new, kr_new, *([cache_ckv] * npg), *([cache_kr] * npg))


def _uv_kernel(o_ref, wuv_ref, y_ref):
    y = _dot(o_ref[0].astype(BF16), wuv_ref[0])
    for h in range(1, MLA_HEADS):
        y = y + _dot(o_ref[h].astype(BF16), wuv_ref[h])
    y_ref[...] = y


def _uv(o_lat, wuv, tm):
    ntok = o_lat.shape[1]
    return pl.pallas_call(
        _uv_kernel,
        grid=(ntok // tm,),
        in_specs=[pl.BlockSpec((MLA_HEADS, tm, KV_LORA), lambda t: (0, t, 0)), _const_spec(wuv.shape)],
        out_specs=pl.BlockSpec((tm, MLA_WIDTH), lambda t: (t, 0)),
        out_shape=jax.ShapeDtypeStruct((ntok, MLA_WIDTH), F32),
        compiler_params=_params(1),
        name="uv",
    )(o_lat, wuv)


def _memkv_kernel(mem_ref, wk_ref, wv_ref, k_ref, v_ref):
    mb = mem_ref[...].astype(BF16)
    k_ref[...] = _dot(mb, wk_ref[...])
    v_ref[...] = _dot(mb, wv_ref[...])


def _memkv(mem2d, wk, wv, tm):
    n = mem2d.shape[0]
    out = pl.BlockSpec((tm, X_WIDTH), lambda t: (t, 0))
    return pl.pallas_call(
        _memkv_kernel,
        grid=(n // tm,),
        in_specs=[pl.BlockSpec((tm, D_MODEL), lambda t: (t, 0)), _const_spec(wk.shape),
                  _const_spec(wv.shape)],
        out_specs=[out, out],
        out_shape=[jax.ShapeDtypeStruct((n, X_WIDTH), F32)] * 2,
        compiler_params=_params(1),
        name="memkv",
    )(mem2d, wk, wv)


def _xattn_kernel(x_ref, wqx_ref, mk_ref, mv_ref, y_ref, *, groups, g):
    qx = _dot(x_ref[...].astype(BF16), wqx_ref[...])
    for s in range(groups):
        rs = slice(s * g, (s + 1) * g)
        for h in range(X_HEADS):
            hs = slice(h * X_HEAD_DIM, (h + 1) * X_HEAD_DIM)
            sc = _dot_nt(qx[rs, hs].astype(BF16), mk_ref[s, :, hs].astype(BF16)) * X_SCALE
            p = jnp.exp(sc - jnp.max(sc, axis=-1, keepdims=True))
            p = p / jnp.sum(p, axis=-1, keepdims=True)
            y_ref[rs, hs] = _dot(p.astype(BF16), mv_ref[s, :, hs].astype(BF16))


def _xattn(x2d, wqx, mem_k, mem_v, groups, g, tiles_per_mem):
    ntok = x2d.shape[0]
    tm = groups * g
    mem = pl.BlockSpec((groups, MEM_LEN, X_WIDTH), lambda t: (t // tiles_per_mem, 0, 0))
    return pl.pallas_call(
        functools.partial(_xattn_kernel, groups=groups, g=g),
        grid=(ntok // tm,),
        in_specs=[pl.BlockSpec((tm, D_MODEL), lambda t: (t, 0)), _const_spec(wqx.shape), mem, mem],
        out_specs=pl.BlockSpec((tm, X_WIDTH), lambda t: (t, 0)),
        out_shape=jax.ShapeDtypeStruct((ntok, X_WIDTH), F32),
        compiler_params=_params(1),
        name="xattn",
    )(x2d, wqx, mem_k, mem_v)


def _merge_kernel(x_ref, ymla_ref, yx_ref, prev_ref, wb_ref, wpm_ref, wpc_ref, wpx_ref, wo_ref,
                  convw_ref, lng_ref, lnb_ref, out_ref, pre_ref, carry_scr,
                  *, tm, seq_rows, tiles_per_seq, alpha):
    x = x_ref[...]
    xb = x.astype(BF16)

    def proj(k):
        lo = k * CONV_DIM if k < 6 else 6 * CONV_DIM + (k - 6) * D_MODEL
        hi = lo + (CONV_DIM if k < 6 else D_MODEL)
        return _dot(xb, wb_ref[:, lo:hi])

    def branch(y, z, gate, wp_ref):
        return jax.nn.sigmoid(gate) * _dot((y * jax.nn.silu(z)).astype(BF16), wp_ref[...])

    m = branch(ymla_ref[...], proj(0), proj(6), wpm_ref)

    pre = proj(2) * proj(3)
    row = lax.broadcasted_iota(jnp.int32, (tm, CONV_DIM), 0)
    if tiles_per_seq is not None:
        @pl.when(pl.program_id(0) % tiles_per_seq == 0)
        def _():
            carry_scr[...] = jnp.zeros(carry_scr.shape, F32)
        prev1 = jnp.broadcast_to(carry_scr[7:8, :], (tm, CONV_DIM))
        prev0 = jnp.broadcast_to(carry_scr[6:7, :], (tm, CONV_DIM))
        pos = row
    else:
        n_seq = tm // seq_rows
        prev = prev_ref[...]
        bc = lambda r: jnp.broadcast_to(prev[:, r:r + 1, :], (n_seq, seq_rows, CONV_DIM)).reshape(tm, CONV_DIM)
        prev0, prev1 = bc(0), bc(1)
        pos = _mod_pow2(row, seq_rows)
    back1 = jnp.where(pos == 0, prev1, pltpu.roll(pre, 1, 0))
    back2 = jnp.where(pos == 0, prev0, jnp.where(pos == 1, prev1, pltpu.roll(pre, 2, 0)))
    cw = convw_ref[...]
    conv = cw[0:1, :] * back2 + cw[1:2, :] * back1 + cw[2:3, :] * pre
    pre_ref[...] = pre
    if tiles_per_seq is not None:
        carry_scr[...] = pre[tm - 8:, :]
    m = m + branch(proj(1) * conv, proj(4), proj(7), wpc_ref)

    m = m + branch(yx_ref[...], proj(5), proj(8), wpx_ref)

    r = alpha * x + _dot(m.astype(BF16), wo_ref[...])
    mu = jnp.mean(r, axis=-1, keepdims=True)
    d = r - mu
    var = jnp.mean(d * d, axis=-1, keepdims=True)
    out_ref[...] = d * lax.rsqrt(var + NORM_EPS) * lng_ref[...] + lnb_ref[...]


def _merge(x2d, ymla, yx, prev, w, tm, seq_rows, tiles_per_seq, alpha):
    ntok = x2d.shape[0]
    tok = lambda width: pl.BlockSpec((tm, width), lambda t: (t, 0))
    n_seq = max(tm // seq_rows, 1)
    prev_spec = pl.BlockSpec((n_seq, CONV_WIDTH - 1, CONV_DIM),
                             lambda t: (t if tiles_per_seq is None else 0, 0, 0))
    weights = [w["wb"], w["wpm"], w["wpc"], w["wpx"], w["wo"], w["convw"], w["lng"], w["lnb"]]
    return pl.pallas_call(
        functools.partial(_merge_kernel, tm=tm, seq_rows=seq_rows, tiles_per_seq=tiles_per_seq,
                          alpha=alpha),
        grid=(ntok // tm,),
        in_specs=[tok(D_MODEL), tok(MLA_WIDTH), tok(X_WIDTH), prev_spec]
        + [_const_spec(a.shape) for a in weights],
        out_specs=[tok(D_MODEL), tok(CONV_DIM)],
        out_shape=[jax.ShapeDtypeStruct((ntok, D_MODEL), F32),
                   jax.ShapeDtypeStruct((ntok, CONV_DIM), F32)],
        scratch_shapes=[pltpu.VMEM((8, CONV_DIM), F32)],
        compiler_params=_params(1),
        name="merge",
    )(x2d, ymla, yx, prev, *weights)


def _swap_halves(w):
    half = ROPE_DIM // 2
    return jnp.concatenate([w[..., half:], w[..., :half]], axis=-1)


def _pack_weights(w_in, q_norm_g, kv_norm_g, w_uq, w_uk, w_uv, conv_w, w_mk, w_mv,
                  w_p_mla, w_p_conv, w_p_x, w_o, ln_g, ln_b):
    (w_cq, w_ckv, w_kr, w_zmla, w_cb, w_cc, w_ch, w_zc, w_qx, w_zx, w_g) = jnp.split(
        w_in, SPLIT_POINTS, axis=-1)
    tile_heads = lambda a: jnp.tile(a, (1, MLA_HEADS))
    wa = jnp.concatenate([w_cq, w_ckv, tile_heads(w_kr), tile_heads(_swap_halves(w_kr))], axis=1)
    uq = w_uq.reshape(Q_LORA, MLA_HEADS, NOPE_DIM + ROPE_DIM)
    uq_rope = uq[:, :, NOPE_DIM:]
    wuq = jnp.concatenate([uq[:, :, :NOPE_DIM].reshape(Q_LORA, -1),
                           uq_rope.reshape(Q_LORA, -1),
                           _swap_halves(uq_rope).reshape(Q_LORA, -1)], axis=1)
    uk_t = jnp.transpose(w_uk, (1, 2, 0))
    zeros = jnp.zeros((MLA_HEADS // 2, NOPE_DIM, KV_LORA), w_uk.dtype)
    wuk = jnp.concatenate([jnp.concatenate([uk_t[0::2], zeros], axis=2),
                           jnp.concatenate([zeros, uk_t[1::2]], axis=2)], axis=1)
    uv_t = jnp.transpose(w_uv, (1, 0, 2))
    wuv = jnp.stack([jnp.pad(uv_t[h], ((0, 0), (h * V_DIM, MLA_WIDTH - (h + 1) * V_DIM)))
                     for h in range(MLA_HEADS)])
    wb = jnp.concatenate([w_zmla, w_cb, w_cc, w_ch, w_zc, w_zx, w_g], axis=1)
    bf = lambda a: a.astype(BF16)
    row = lambda a: a.reshape(1, -1).astype(F32)
    return dict(wa=bf(wa), gq=row(q_norm_g), gkv=row(kv_norm_g), wuq=bf(wuq), wuk=bf(wuk),
                wuv=bf(wuv), wqx=bf(w_qx), wmk=bf(w_mk), wmv=bf(w_mv), wb=bf(wb),
                wpm=bf(w_p_mla), wpc=bf(w_p_conv), wpx=bf(w_p_x), wo=bf(w_o),
                convw=conv_w.astype(F32), lng=row(ln_g), lnb=row(ln_b))


def _rope_tables(pos):
    half = ROPE_DIM // 2
    inv_freq = ROPE_THETA ** (-jnp.arange(half, dtype=jnp.float32) * (2.0 / ROPE_DIM))
    ang = pos.astype(jnp.float32)[:, None] * inv_freq[None, :]
    cos, sin = jnp.cos(ang), jnp.sin(ang)
    return (jnp.tile(jnp.concatenate([cos, cos], axis=-1), (1, MLA_HEADS)),
            jnp.tile(jnp.concatenate([-sin, sin], axis=-1), (1, MLA_HEADS)))


def kernel(x_prompt, x_sample, mem_prompt, cache_ckv, cache_kr, cache_conv, cache_mem_k,
           cache_mem_v, page_table, w_in, q_norm_g, kv_norm_g, w_uq, w_uk, w_uv, conv_w,
           w_mk, w_mv, w_p_mla, w_p_conv, w_p_x, w_o, ln_g, ln_b):
    depth = w_in.shape[0]
    batch, seq, _ = x_prompt.shape
    dec_batch, dec_seq, _ = x_sample.shape
    past_len = page_table.shape[1] * PAGE_SIZE
    alpha = (2 * depth) ** 0.25
    tm_p, tq, tm_s, x_groups_s, npg = 512, 256, 256, 8, 16

    cos_p, sin_p = _rope_tables(jnp.arange(seq, dtype=jnp.float32))
    cos_s, sin_s = _rope_tables(past_len + jnp.arange(dec_seq, dtype=jnp.float32))
    cos_s = jnp.tile(cos_s, (tm_s // dec_seq, 1))
    sin_s = jnp.tile(sin_s, (tm_s // dec_seq, 1))

    hp = x_prompt.reshape(batch * seq, D_MODEL)
    hs = x_sample.reshape(dec_batch * dec_seq, D_MODEL)
    outs = [[] for _ in range(8)]
    for l in range(depth):
        w = _pack_weights(w_in[l], q_norm_g[l], kv_norm_g[l], w_uq[l], w_uk[l], w_uv[l], conv_w[l],
                          w_mk[l], w_mv[l], w_p_mla[l], w_p_conv[l], w_p_x[l], w_o[l], ln_g[l], ln_b[l])
        qlat, qrope, kcat, ckv_p, kr_p = _qkv(hp, w, cos_p, sin_p, tm_p, seq // tm_p, BF16)
        ymla = _mla_prompt(qlat, qrope, kcat, w["wuv"], batch, seq, tq)
        memk, memv = _memkv(mem_prompt.reshape(batch * MEM_LEN, D_MODEL), w["wmk"], w["wmv"], MEM_LEN)
        yx = _xattn(hp, w["wqx"], memk.reshape(batch, MEM_LEN, X_WIDTH),
                    memv.reshape(batch, MEM_LEN, X_WIDTH), 1, tm_p, seq // tm_p)
        no_prev = jnp.zeros((1, CONV_WIDTH - 1, CONV_DIM), F32)
        hp, pre_p = _merge(hp, ymla, yx, no_prev, w, tm_p, seq, seq // tm_p, alpha)
        qlat, qrope, _, ckv_s, kr_s = _qkv(hs, w, cos_s, sin_s, tm_s, 1, F32)
        o_lat = _mla_decode(page_table, qlat, qrope, ckv_s, kr_s, cache_ckv[l], cache_kr[l], dec_seq, npg)
        ymla = _uv(o_lat, w["wuv"], tm_s)
        yx = _xattn(hs, w["wqx"], cache_mem_k[l].reshape(dec_batch, MEM_LEN, X_WIDTH),
                    cache_mem_v[l].reshape(dec_batch, MEM_LEN, X_WIDTH), x_groups_s, dec_seq, 1)
        hs, pre_s = _merge(hs, ymla, yx, cache_conv[l], w, tm_s, dec_seq, None, alpha)

        keep = CONV_WIDTH - 1
        for lst, val in zip(outs, (
                ckv_p.reshape(batch, seq, KV_LORA), kr_p.reshape(batch, seq, ROPE_DIM),
                pre_p.reshape(batch, seq, CONV_DIM)[:, seq - keep:],
                memk.reshape(batch, MEM_LEN, X_HEADS, X_HEAD_DIM),
                memv.reshape(batch, MEM_LEN, X_HEADS, X_HEAD_DIM),
                ckv_s.reshape(dec_batch, dec_seq, KV_LORA), kr_s.reshape(dec_batch, dec_seq, ROPE_DIM),
                pre_s.reshape(dec_batch, dec_seq, CONV_DIM)[:, dec_seq - keep:])):
            lst.append(val)
    return (hp.reshape(batch, seq, D_MODEL), hs.reshape(dec_batch, dec_seq, D_MODEL),
            *[jnp.stack(lst) for lst in outs])
```

```python
import functools
import math

import jax
import jax.numpy as jnp
import numpy as np
from jax import lax
from jax.experimental import pallas as pl
from jax.experimental.pallas import tpu as pltpu

D_MODEL = 1024
PAGE_SIZE = 128
MLA_HEADS = 8
NOPE_DIM = 64
ROPE_DIM = 32
V_DIM = 64
Q_LORA = 384
KV_LORA = 256
MLA_WIDTH = MLA_HEADS * V_DIM
MLA_SCALE = (NOPE_DIM + ROPE_DIM) ** -0.5
ROPE_THETA = 10000.0
CONV_DIM = D_MODEL // 2
CONV_WIDTH = 3
MEM_LEN = 256
X_HEADS = 4
X_HEAD_DIM = 128
X_WIDTH = X_HEADS * X_HEAD_DIM
X_SCALE = X_HEAD_DIM ** -0.5
N_BRANCH = 3
NORM_EPS = 1e-6
NEG_INF = -1e30
SPLIT_SIZES = (Q_LORA, KV_LORA, ROPE_DIM, MLA_WIDTH,
               CONV_DIM, CONV_DIM, CONV_DIM, CONV_DIM,
               X_WIDTH, X_WIDTH, N_BRANCH * D_MODEL)
SPLIT_POINTS = tuple(int(v) for v in np.cumsum(SPLIT_SIZES)[:-1])

ROPE_LANES = MLA_HEADS * ROPE_DIM
KCAT = KV_LORA + ROPE_LANES
LANES = 128
KT_TILE = 256
VMEM_LIMIT = 48 * 1024 * 1024
BF16 = jnp.bfloat16
F32 = jnp.float32
NT_DIMS = (((1,), (1,)), ((), ()))
MLA_EXP2_SCALE = MLA_SCALE * math.log2(math.e)


def _dot(a, b):
    return jnp.dot(a, b, preferred_element_type=F32)


def _dot_nt(a, b):
    return lax.dot_general(a, b, NT_DIMS, preferred_element_type=F32)


def _rms(x, g):
    return x * lax.rsqrt(jnp.mean(x * x, axis=-1, keepdims=True) + NORM_EPS) * g


def _const_spec(shape):
    return pl.BlockSpec(shape, lambda *_: (0,) * len(shape), pipeline_mode=pl.Buffered(1))


def _mod_pow2(x, n):
    assert n & (n - 1) == 0
    return x & (n - 1)


def _params(n_axes):
    return pltpu.CompilerParams(dimension_semantics=("arbitrary",) * n_axes,
                                vmem_limit_bytes=VMEM_LIMIT)


def _qkv_kernel(x_ref, wa_ref, gq_ref, gkv_ref, wuq_ref, wuk_ref, cos_ref, sin_ref,
                qlat_ref, qrope_ref, kt_ref, v_ref, ckv_ref, kr_ref):
    cos = cos_ref[...]
    sin = sin_ref[...]
    xb = x_ref[...].astype(BF16)
    pa = _dot(xb, wa_ref[...])
    cqn = _rms(pa[:, :Q_LORA], gq_ref[...])
    qa = _dot(cqn.astype(BF16), wuq_ref[...])
    r0 = MLA_HEADS * NOPE_DIM
    qrope_ref[...] = qa[:, r0:r0 + ROPE_LANES] * cos + qa[:, r0 + ROPE_LANES:] * sin
    for p in range(MLA_HEADS // 2):
        ql = _dot(qa[:, 128 * p:128 * (p + 1)].astype(BF16), wuk_ref[p])
        qlat_ref[2 * p] = ql[:, :KV_LORA].astype(qlat_ref.dtype)
        qlat_ref[2 * p + 1] = ql[:, KV_LORA:].astype(qlat_ref.dtype)
    ckvn = _rms(pa[:, Q_LORA:Q_LORA + KV_LORA], gkv_ref[...])
    ckv_ref[...] = ckvn
    k0 = Q_LORA + KV_LORA
    krt = pa[:, k0:k0 + ROPE_LANES] * cos + pa[:, k0 + ROPE_LANES:] * sin
    kr_ref[...] = krt[:, :ROPE_DIM]
    v_ref[...] = ckvn.astype(BF16)
    for t in range(kt_ref.shape[0]):
        rows = slice(t * KT_TILE, (t + 1) * KT_TILE)
        kt_ref[t, :KV_LORA, :] = ckvn[rows, :].T.astype(BF16)
        kt_ref[t, KV_LORA:, :] = krt[rows, :].T.astype(BF16)


def _qkv(x2d, w, cos, sin, tm, table_tiles, q_dtype):
    ntok = x2d.shape[0]
    tok = lambda width: pl.BlockSpec((tm, width), lambda t: (t, 0))
    tab = pl.BlockSpec((tm, ROPE_LANES), lambda t: (t % table_tiles, 0))
    return pl.pallas_call(
        _qkv_kernel,
        grid=(ntok // tm,),
        in_specs=[tok(D_MODEL), _const_spec(w["wa"].shape), _const_spec((1, Q_LORA)),
                  _const_spec((1, KV_LORA)), _const_spec(w["wuq"].shape),
                  _const_spec(w["wuk"].shape), tab, tab],
        out_specs=[pl.BlockSpec((MLA_HEADS, tm, KV_LORA), lambda t: (0, t, 0)),
                   tok(ROPE_LANES),
                   pl.BlockSpec((tm // KT_TILE, KCAT, KT_TILE), lambda t: (t, 0, 0)),
                   tok(KV_LORA), tok(KV_LORA), tok(ROPE_DIM)],
        out_shape=[jax.ShapeDtypeStruct((MLA_HEADS, ntok, KV_LORA), q_dtype),
                   jax.ShapeDtypeStruct((ntok, ROPE_LANES), F32),
                   jax.ShapeDtypeStruct((ntok // KT_TILE, KCAT, KT_TILE), BF16),
                   jax.ShapeDtypeStruct((ntok, KV_LORA), BF16),
                   jax.ShapeDtypeStruct((ntok, KV_LORA), F32),
                   jax.ShapeDtypeStruct((ntok, ROPE_DIM), F32)],
        compiler_params=_params(1),
        name="qkv",
    )(x2d, w["wa"], w["gq"], w["gkv"], w["wuq"], w["wuk"], cos, sin)


def _lane_repeat(x, width):
    return jnp.concatenate([x] * (width // LANES), axis=-1)


def _softmax_weights(s, m_old, l_old):
    m_new = jnp.maximum(m_old, jnp.max(s, axis=-1, keepdims=True))
    alpha = jnp.exp2((m_old - m_new) * MLA_EXP2_SCALE)
    p = jnp.exp2((s - _lane_repeat(m_new, s.shape[-1])) * MLA_EXP2_SCALE)
    l_new = alpha * l_old + jnp.sum(p, axis=-1, keepdims=True)
    return m_new, l_new, alpha, p.astype(BF16)


def _softmax_update(s, v_fn, m_old, l_old, acc_old):
    m_new, l_new, alpha, p = _softmax_weights(s, m_old, l_old)
    return m_new, l_new, _lane_repeat(alpha, acc_old.shape[-1]) * acc_old + v_fn(p)


def _softmax_init(m_scr, l_scr, acc_scr):
    m_scr[...] = jnp.full(m_scr.shape, -jnp.inf, F32)
    l_scr[...] = jnp.zeros(l_scr.shape, F32)
    acc_scr[...] = jnp.zeros(acc_scr.shape, F32)


def _mla_prompt_kernel(qlat_ref, qrope_ref, kt_ref, v_ref, wuv_ref, y_ref,
                       q_scr, m_scr, l_scr, acc_scr, *, tq, rq):
    assert tq == KT_TILE and tq % rq == 0
    i = pl.program_id(1)
    qr = qrope_ref[...]
    lane = lax.broadcasted_iota(jnp.int32, (tq, ROPE_LANES), 1)
    for h in range(MLA_HEADS):
        q_scr[h, :, :KV_LORA] = qlat_ref[h]
        own = (lane >= h * ROPE_DIM) & (lane < (h + 1) * ROPE_DIM)
        q_scr[h, :, KV_LORA:] = jnp.where(own, qr, 0.0).astype(BF16)
    _softmax_init(m_scr, l_scr, acc_scr)

    def step(j, diagonal):
        kt = kt_ref[j]
        v = v_ref[pl.ds(pl.multiple_of(j * tq, tq), tq), :]
        chunks = [(h, r0) for h in range(MLA_HEADS) for r0 in range(0, tq, rq)]
        scores = [_dot(q_scr[h, r0:r0 + rq, :], kt) for h, r0 in chunks]
        weights = []
        for (h, r0), s in zip(chunks, scores):
            rs = slice(r0, r0 + rq)
            if diagonal:
                q_pos = r0 + lax.broadcasted_iota(jnp.int32, (rq, tq), 0)
                k_pos = lax.broadcasted_iota(jnp.int32, (rq, tq), 1)
                s = jnp.where(k_pos <= q_pos, s, NEG_INF)
            m_scr[h, rs, :], l_scr[h, rs, :], alpha, p = _softmax_weights(
                s, m_scr[h, rs, :], l_scr[h, rs, :])
            weights.append((alpha, p))
        for (h, r0), (alpha, p) in zip(chunks, weights):
            rs = slice(r0, r0 + rq)
            acc_scr[h, rs, :] = _lane_repeat(alpha, KV_LORA) * acc_scr[h, rs, :] + _dot(p, v)

    def body(j, carry):
        step(j, False)
        return carry

    lax.fori_loop(0, i, body, 0)
    step(i, True)
    y = jnp.zeros((tq, MLA_WIDTH), F32)
    for h in range(MLA_HEADS):
        o = acc_scr[h] * _lane_repeat(1.0 / l_scr[h], KV_LORA)
        y = y + _dot(o.astype(BF16), wuv_ref[h])
    y_ref[...] = y


def _mla_prompt(qlat, qrope, kt, v, wuv, batch, seq, tq):
    ntok = batch * seq
    nq = seq // tq
    return pl.pallas_call(
        functools.partial(_mla_prompt_kernel, tq=tq, rq=128),
        grid=(batch, nq),
        in_specs=[pl.BlockSpec((MLA_HEADS, tq, KV_LORA), lambda b, i: (0, b * nq + i, 0)),
                  pl.BlockSpec((tq, ROPE_LANES), lambda b, i: (b * nq + i, 0)),
                  pl.BlockSpec((seq // KT_TILE, KCAT, KT_TILE), lambda b, i: (b, 0, 0)),
                  pl.BlockSpec((seq, KV_LORA), lambda b, i: (b, 0)),
                  _const_spec(wuv.shape)],
        out_specs=pl.BlockSpec((tq, MLA_WIDTH), lambda b, i: (b * nq + i, 0)),
        out_shape=jax.ShapeDtypeStruct((ntok, MLA_WIDTH), F32),
        scratch_shapes=[pltpu.VMEM((MLA_HEADS, tq, KCAT), BF16),
                        pltpu.VMEM((MLA_HEADS, tq, LANES), F32),
                        pltpu.VMEM((MLA_HEADS, tq, LANES), F32),
                        pltpu.VMEM((MLA_HEADS, tq, KV_LORA), F32)],
        compiler_params=_params(2),
        name="mla_prompt",
    )(qlat, qrope, kt, v, wuv)


def _mla_decode_kernel(pt_ref, qlat_ref, qrope_ref, ckvn_ref, krn_ref, ckv_hbm, krt_hbm, o_ref,
                       ckv_buf, krt_buf, sem, ql_scr, qr_scr, kn_scr, krn_scr,
                       m_scr, l_scr, acc_scr, *, cpg, n_chunks, dec_seq):
    assert n_chunks % 2 == 0
    seq = pl.program_id(0)
    rows = MLA_HEADS * dec_seq

    def page_copies(sq, chunk, slot):
        copies = []
        for k in range(cpg):
            page = pt_ref[sq, chunk * cpg + k]
            copies.append(pltpu.make_async_copy(ckv_hbm.at[page], ckv_buf.at[slot, k], sem.at[slot, 0]))
            copies.append(pltpu.make_async_copy(krt_hbm.at[page], krt_buf.at[slot, k], sem.at[slot, 1]))
        return copies

    def start(sq, chunk, slot):
        for cp in page_copies(sq, chunk, slot):
            cp.start()

    @pl.when(seq == 0)
    def _():
        start(0, 0, 0)

    ql_scr[...] = jnp.concatenate([qlat_ref[h] for h in range(MLA_HEADS)], axis=0).astype(BF16)
    qr = qrope_ref[...]
    qr_scr[...] = jnp.concatenate(
        [qr[:, ROPE_DIM * h:ROPE_DIM * (h + 1)] for h in range(MLA_HEADS)], axis=0).astype(BF16)
    kn_scr[...] = jnp.zeros(kn_scr.shape, F32)
    krn_scr[...] = jnp.zeros(krn_scr.shape, F32)
    kn_scr[:dec_seq, :] = ckvn_ref[...]
    krn_scr[:dec_seq, :] = krn_ref[...]
    _softmax_init(m_scr, l_scr, acc_scr)

    def chunk_body(c, carry):
        slot = c & 1

        @pl.when(c + 1 < n_chunks)
        def _():
            start(seq, c + 1, 1 - slot)

        @pl.when((c + 1 == n_chunks) & (seq + 1 < pl.num_programs(0)))
        def _():
            start(seq + 1, 0, 1 - slot)

        for cp in page_copies(seq, c, slot):
            cp.wait()
        ql = ql_scr[...]
        qr_h = qr_scr[...]
        cks = [ckv_buf[slot, k].astype(BF16) for k in range(cpg)]
        s = jnp.concatenate(
            [_dot_nt(ql, cks[k]) + _dot(qr_h, krt_buf[slot, k].astype(BF16)) for k in range(cpg)],
            axis=1)

        def pv(p):
            out = _dot(p[:, :PAGE_SIZE], cks[0])
            for k in range(1, cpg):
                out = out + _dot(p[:, PAGE_SIZE * k:PAGE_SIZE * (k + 1)], cks[k])
            return out

        m_scr[...], l_scr[...], acc_scr[...] = _softmax_update(
            s, pv, m_scr[...], l_scr[...], acc_scr[...])
        return carry

    lax.fori_loop(0, n_chunks, chunk_body, 0)

    ck = kn_scr[...].astype(BF16)
    s_new = _dot_nt(ql_scr[...], ck) + _dot_nt(qr_scr[...], krn_scr[...].astype(BF16))
    q_pos = _mod_pow2(lax.broadcasted_iota(jnp.int32, (rows, PAGE_SIZE), 0), dec_seq)
    k_pos = lax.broadcasted_iota(jnp.int32, (rows, PAGE_SIZE), 1)
    s_new = jnp.where(k_pos <= q_pos, s_new, NEG_INF)
    _, l_fin, acc_fin = _softmax_update(s_new, lambda p: _dot(p, ck),
                                        m_scr[...], l_scr[...], acc_scr[...])
    o = acc_fin * _lane_repeat(1.0 / l_fin, KV_LORA)
    for h in range(MLA_HEADS):
        o_ref[h] = o[h * dec_seq:(h + 1) * dec_seq, :]


def _mla_decode(page_table, qlat, qrope, ckv_new, kr_new, cache_ckv, cache_krt, dec_seq, cpg):
    n_seq, n_pages = page_table.shape
    ntok = n_seq * dec_seq
    rows = MLA_HEADS * dec_seq
    tok = lambda width: pl.BlockSpec((dec_seq, width), lambda s, pt: (s, 0))
    head_tok = pl.BlockSpec((MLA_HEADS, dec_seq, KV_LORA), lambda s, pt: (0, s, 0))
    hbm = pl.BlockSpec(memory_space=pl.ANY)
    grid_spec = pltpu.PrefetchScalarGridSpec(
        num_scalar_prefetch=1,
        grid=(n_seq,),
        in_specs=[head_tok, tok(ROPE_LANES), tok(KV_LORA), tok(ROPE_DIM), hbm, hbm],
        out_specs=head_tok,
        scratch_shapes=[pltpu.VMEM((2, cpg, PAGE_SIZE, KV_LORA), F32),
                        pltpu.VMEM((2, cpg, ROPE_DIM, PAGE_SIZE), F32),
                        pltpu.SemaphoreType.DMA((2, 2)),
                        pltpu.VMEM((rows, KV_LORA), BF16), pltpu.VMEM((rows, ROPE_DIM), BF16),
                        pltpu.VMEM((PAGE_SIZE, KV_LORA), F32), pltpu.VMEM((PAGE_SIZE, ROPE_DIM), F32),
                        pltpu.VMEM((rows, LANES), F32), pltpu.VMEM((rows, LANES), F32),
                        pltpu.VMEM((rows, KV_LORA), F32)],
    )
    return pl.pallas_call(
        functools.partial(_mla_decode_kernel, cpg=cpg, n_chunks=n_pages // cpg, dec_seq=dec_seq),
        grid_spec=grid_spec,
        out_shape=jax.ShapeDtypeStruct((MLA_HEADS, ntok, KV_LORA), F32),
        compiler_params=_params(1),
        name="mla_decode",
    )(page_table, qlat, qrope, ckv_new, kr_new, cache_ckv, cache_krt)


def _uv_kernel(o_ref, wuv_ref, y_ref):
    y = _dot(o_ref[0].astype(BF16), wuv_ref[0])
    for h in range(1, MLA_HEADS):
        y = y + _dot(o_ref[h].astype(BF16), wuv_ref[h])
    y_ref[...] = y


def _uv(o_lat, wuv, tm):
    ntok = o_lat.shape[1]
    return pl.pallas_call(
        _uv_kernel,
        grid=(ntok // tm,),
        in_specs=[pl.BlockSpec((MLA_HEADS, tm, KV_LORA), lambda t: (0, t, 0)), _const_spec(wuv.shape)],
        out_specs=pl.BlockSpec((tm, MLA_WIDTH), lambda t: (t, 0)),
        out_shape=jax.ShapeDtypeStruct((ntok, MLA_WIDTH), F32),
        compiler_params=_params(1),
        name="uv",
    )(o_lat, wuv)


def _memkv_kernel(mem_ref, wk_ref, wv_ref, k_ref, v_ref):
    mb = mem_ref[...].astype(BF16)
    k_ref[...] = _dot(mb, wk_ref[...])
    v_ref[...] = _dot(mb, wv_ref[...])


def _memkv(mem2d, wk, wv, tm):
    n = mem2d.shape[0]
    out = pl.BlockSpec((tm, X_WIDTH), lambda t: (t, 0))
    return pl.pallas_call(
        _memkv_kernel,
        grid=(n // tm,),
        in_specs=[pl.BlockSpec((tm, D_MODEL), lambda t: (t, 0)), _const_spec(wk.shape),
                  _const_spec(wv.shape)],
        out_specs=[out, out],
        out_shape=[jax.ShapeDtypeStruct((n, X_WIDTH), F32)] * 2,
        compiler_params=_params(1),
        name="memkv",
    )(mem2d, wk, wv)


def _xattn_kernel(x_ref, wqx_ref, mk_ref, mv_ref, y_ref, *, groups, g):
    qx = _dot(x_ref[...].astype(BF16), wqx_ref[...])
    for s in range(groups):
        rs = slice(s * g, (s + 1) * g)
        for h in range(X_HEADS):
            hs = slice(h * X_HEAD_DIM, (h + 1) * X_HEAD_DIM)
            sc = _dot_nt(qx[rs, hs].astype(BF16), mk_ref[s, :, h, :].astype(BF16)) * X_SCALE
            p = jnp.exp(sc - jnp.max(sc, axis=-1, keepdims=True))
            p = p / jnp.sum(p, axis=-1, keepdims=True)
            y_ref[rs, hs] = _dot(p.astype(BF16), mv_ref[s, :, h, :].astype(BF16))


def _xattn(x2d, wqx, mem_k, mem_v, groups, g, tiles_per_mem):
    ntok = x2d.shape[0]
    tm = groups * g
    mem = pl.BlockSpec((groups, MEM_LEN, X_HEADS, X_HEAD_DIM),
                       lambda t: (t // tiles_per_mem, 0, 0, 0))
    return pl.pallas_call(
        functools.partial(_xattn_kernel, groups=groups, g=g),
        grid=(ntok // tm,),
        in_specs=[pl.BlockSpec((tm, D_MODEL), lambda t: (t, 0)), _const_spec(wqx.shape), mem, mem],
        out_specs=pl.BlockSpec((tm, X_WIDTH), lambda t: (t, 0)),
        out_shape=jax.ShapeDtypeStruct((ntok, X_WIDTH), F32),
        compiler_params=_params(1),
        name="xattn",
    )(x2d, wqx, mem_k, mem_v)


def _merge_kernel(x_ref, ymla_ref, yx_ref, prev_ref, wb_ref, wpm_ref, wpc_ref, wpx_ref, wo_ref,
                  convw_ref, lng_ref, lnb_ref, out_ref, pre_ref, carry_scr,
                  *, tm, seq_rows, tiles_per_seq, alpha):
    x = x_ref[...]
    xb = x.astype(BF16)

    def proj(k):
        lo = k * CONV_DIM if k < 6 else 6 * CONV_DIM + (k - 6) * D_MODEL
        hi = lo + (CONV_DIM if k < 6 else D_MODEL)
        return _dot(xb, wb_ref[:, lo:hi])

    def branch(y, z, gate, wp_ref):
        return jax.nn.sigmoid(gate) * _dot((y * jax.nn.silu(z)).astype(BF16), wp_ref[...])

    m = branch(ymla_ref[...], proj(0), proj(6), wpm_ref)

    pre = proj(2) * proj(3)
    row = lax.broadcasted_iota(jnp.int32, (tm, CONV_DIM), 0)
    if tiles_per_seq is not None:
        @pl.when(pl.program_id(0) % tiles_per_seq == 0)
        def _():
            carry_scr[...] = jnp.zeros(carry_scr.shape, F32)
        prev1 = jnp.broadcast_to(carry_scr[7:8, :], (tm, CONV_DIM))
        prev0 = jnp.broadcast_to(carry_scr[6:7, :], (tm, CONV_DIM))
        pos = row
    else:
        n_seq = tm // seq_rows
        prev = prev_ref[...]
        bc = lambda r: jnp.broadcast_to(prev[:, r:r + 1, :], (n_seq, seq_rows, CONV_DIM)).reshape(tm, CONV_DIM)
        prev0, prev1 = bc(0), bc(1)
        pos = _mod_pow2(row, seq_rows)
    back1 = jnp.where(pos == 0, prev1, pltpu.roll(pre, 1, 0))
    back2 = jnp.where(pos == 0, prev0, jnp.where(pos == 1, prev1, pltpu.roll(pre, 2, 0)))
    cw = convw_ref[...]
    conv = cw[0:1, :] * back2 + cw[1:2, :] * back1 + cw[2:3, :] * pre
    pre_ref[...] = pre
    if tiles_per_seq is not None:
        carry_scr[...] = pre[tm - 8:, :]
    m = m + branch(proj(1) * conv, proj(4), proj(7), wpc_ref)

    m = m + branch(yx_ref[...], proj(5), proj(8), wpx_ref)

    r = alpha * x + _dot(m.astype(BF16), wo_ref[...])
    mu = jnp.mean(r, axis=-1, keepdims=True)
    d = r - mu
    var = jnp.mean(d * d, axis=-1, keepdims=True)
    out_ref[...] = d * lax.rsqrt(var + NORM_EPS) * lng_ref[...] + lnb_ref[...]


def _merge(x2d, ymla, yx, prev, w, tm, seq_rows, tiles_per_seq, alpha):
    ntok = x2d.shape[0]
    tok = lambda width: pl.BlockSpec((tm, width), lambda t: (t, 0))
    n_seq = max(tm // seq_rows, 1)
    prev_spec = pl.BlockSpec((n_seq, CONV_WIDTH - 1, CONV_DIM),
                             lambda t: (t if tiles_per_seq is None else 0, 0, 0))
    weights = [w["wb"], w["wpm"], w["wpc"], w["wpx"], w["wo"], w["convw"], w["lng"], w["lnb"]]
    return pl.pallas_call(
        functools.partial(_merge_kernel, tm=tm, seq_rows=seq_rows, tiles_per_seq=tiles_per_seq,
                          alpha=alpha),
        grid=(ntok // tm,),
        in_specs=[tok(D_MODEL), tok(MLA_WIDTH), tok(X_WIDTH), prev_spec]
        + [_const_spec(a.shape) for a in weights],
        out_specs=[tok(D_MODEL), tok(CONV_DIM)],
        out_shape=[jax.ShapeDtypeStruct((ntok, D_MODEL), F32),
                   jax.ShapeDtypeStruct((ntok, CONV_DIM), F32)],
        scratch_shapes=[pltpu.VMEM((8, CONV_DIM), F32)],
        compiler_params=_params(1),
        name="merge",
    )(x2d, ymla, yx, prev, *weights)


def _swap_halves(w):
    half = ROPE_DIM // 2
    return jnp.concatenate([w[..., half:], w[..., :half]], axis=-1)


def _pack_weights(w_in, q_norm_g, kv_norm_g, w_uq, w_uk, w_uv, conv_w, w_mk, w_mv,
                  w_p_mla, w_p_conv, w_p_x, w_o, ln_g, ln_b):
    (w_cq, w_ckv, w_kr, w_zmla, w_cb, w_cc, w_ch, w_zc, w_qx, w_zx, w_g) = jnp.split(
        w_in, SPLIT_POINTS, axis=-1)
    tile_heads = lambda a: jnp.tile(a, (1, MLA_HEADS))
    wa = jnp.concatenate([w_cq, w_ckv, tile_heads(w_kr), tile_heads(_swap_halves(w_kr))], axis=1)
    uq = w_uq.reshape(Q_LORA, MLA_HEADS, NOPE_DIM + ROPE_DIM)
    uq_rope = uq[:, :, NOPE_DIM:]
    wuq = jnp.concatenate([uq[:, :, :NOPE_DIM].reshape(Q_LORA, -1),
                           uq_rope.reshape(Q_LORA, -1),
                           _swap_halves(uq_rope).reshape(Q_LORA, -1)], axis=1)
    uk_t = jnp.transpose(w_uk, (1, 2, 0))
    zeros = jnp.zeros((MLA_HEADS // 2, NOPE_DIM, KV_LORA), w_uk.dtype)
    wuk = jnp.concatenate([jnp.concatenate([uk_t[0::2], zeros], axis=2),
                           jnp.concatenate([zeros, uk_t[1::2]], axis=2)], axis=1)
    uv_t = jnp.transpose(w_uv, (1, 0, 2))
    wuv = jnp.stack([jnp.pad(uv_t[h], ((0, 0), (h * V_DIM, MLA_WIDTH - (h + 1) * V_DIM)))
                     for h in range(MLA_HEADS)])
    wb = jnp.concatenate([w_zmla, w_cb, w_cc, w_ch, w_zc, w_zx, w_g], axis=1)
    bf = lambda a: a.astype(BF16)
    row = lambda a: a.reshape(1, -1).astype(F32)
    return dict(wa=bf(wa), gq=row(q_norm_g), gkv=row(kv_norm_g), wuq=bf(wuq), wuk=bf(wuk),
                wuv=bf(wuv), wqx=bf(w_qx), wmk=bf(w_mk), wmv=bf(w_mv), wb=bf(wb),
                wpm=bf(w_p_mla), wpc=bf(w_p_conv), wpx=bf(w_p_x), wo=bf(w_o),
                convw=conv_w.astype(F32), lng=row(ln_g), lnb=row(ln_b))


def _rope_tables(pos):
    half = ROPE_DIM // 2
    inv_freq = ROPE_THETA ** (-jnp.arange(half, dtype=jnp.float32) * (2.0 / ROPE_DIM))
    ang = pos.astype(jnp.float32)[:, None] * inv_freq[None, :]
    cos, sin = jnp.cos(ang), jnp.sin(ang)
    return (jnp.tile(jnp.concatenate([cos, cos], axis=-1), (1, MLA_HEADS)),
            jnp.tile(jnp.concatenate([-sin, sin], axis=-1), (1, MLA_HEADS)))


def kernel(x_prompt, x_sample, mem_prompt, cache_ckv, cache_kr, cache_conv, cache_mem_k,
           cache_mem_v, page_table, w_in, q_norm_g, kv_norm_g, w_uq, w_uk, w_uv, conv_w,
           w_mk, w_mv, w_p_mla, w_p_conv, w_p_x, w_o, ln_g, ln_b):
    depth = w_in.shape[0]
    batch, seq, _ = x_prompt.shape
    dec_batch, dec_seq, _ = x_sample.shape
    past_len = page_table.shape[1] * PAGE_SIZE
    alpha = (2 * depth) ** 0.25
    tm_p, tq, tm_s, x_groups_s, cpg = 512, KT_TILE, 256, 8, 16

    cos_p, sin_p = _rope_tables(jnp.arange(seq, dtype=jnp.float32))
    cos_s, sin_s = _rope_tables(past_len + jnp.arange(dec_seq, dtype=jnp.float32))
    cos_s = jnp.tile(cos_s, (tm_s // dec_seq, 1))
    sin_s = jnp.tile(sin_s, (tm_s // dec_seq, 1))

    hp = x_prompt.reshape(batch * seq, D_MODEL)
    hs = x_sample.reshape(dec_batch * dec_seq, D_MODEL)
    mem_shape = (MEM_LEN, X_HEADS, X_HEAD_DIM)
    outs = [[] for _ in range(8)]
    for l in range(depth):
        w = _pack_weights(w_in[l], q_norm_g[l], kv_norm_g[l], w_uq[l], w_uk[l], w_uv[l], conv_w[l],
                          w_mk[l], w_mv[l], w_p_mla[l], w_p_conv[l], w_p_x[l], w_o[l], ln_g[l], ln_b[l])
        qlat, qrope, kt, v, ckv_p, kr_p = _qkv(hp, w, cos_p, sin_p, tm_p, seq // tm_p, BF16)
        ymla = _mla_prompt(qlat, qrope, kt, v, w["wuv"], batch, seq, tq)
        memk, memv = _memkv(mem_prompt.reshape(batch * MEM_LEN, D_MODEL), w["wmk"], w["wmv"], MEM_LEN)
        memk = memk.reshape(batch, *mem_shape)
        memv = memv.reshape(batch, *mem_shape)
        yx = _xattn(hp, w["wqx"], memk, memv, 1, tm_p, seq // tm_p)
        no_prev = jnp.zeros((1, CONV_WIDTH - 1, CONV_DIM), F32)
        hp, pre_p = _merge(hp, ymla, yx, no_prev, w, tm_p, seq, seq // tm_p, alpha)
        qlat, qrope, _, _, ckv_s, kr_s = _qkv(hs, w, cos_s, sin_s, tm_s, 1, F32)
        o_lat = _mla_decode(page_table, qlat, qrope, ckv_s, kr_s, cache_ckv[l],
                            jnp.swapaxes(cache_kr[l], 1, 2), dec_seq, cpg)
        ymla = _uv(o_lat, w["wuv"], tm_s)
        yx = _xattn(hs, w["wqx"], cache_mem_k[l], cache_mem_v[l], x_groups_s, dec_seq, 1)
        hs, pre_s = _merge(hs, ymla, yx, cache_conv[l], w, tm_s, dec_seq, None, alpha)

        keep = CONV_WIDTH - 1
        for lst, val in zip(outs, (
                ckv_p.reshape(batch, seq, KV_LORA), kr_p.reshape(batch, seq, ROPE_DIM),
                pre_p.reshape(batch, seq, CONV_DIM)[:, seq - keep:],
                memk, memv,
                ckv_s.reshape(dec_batch, dec_seq, KV_LORA), kr_s.reshape(dec_batch, dec_seq, ROPE_DIM),
                pre_s.reshape(dec_batch, dec_seq, CONV_DIM)[:, dec_seq - keep:])):
            lst.append(val)
    return (hp.reshape(batch, seq, D_MODEL), hs.reshape(dec_batch, dec_seq, D_MODEL),
            *[jnp.stack(lst) for lst in outs])
```

```python
import functools
import math

import jax
import jax.numpy as jnp
import numpy as np
from jax import lax
from jax.experimental import pallas as pl
from jax.experimental.pallas import tpu as pltpu

D_MODEL = 1024
PAGE_SIZE = 128
MLA_HEADS = 8
NOPE_DIM = 64
ROPE_DIM = 32
V_DIM = 64
Q_LORA = 384
KV_LORA = 256
MLA_WIDTH = MLA_HEADS * V_DIM
MLA_SCALE = (NOPE_DIM + ROPE_DIM) ** -0.5
ROPE_THETA = 10000.0
CONV_DIM = D_MODEL // 2
CONV_WIDTH = 3
MEM_LEN = 256
X_HEADS = 4
X_HEAD_DIM = 128
X_WIDTH = X_HEADS * X_HEAD_DIM
X_SCALE = X_HEAD_DIM ** -0.5
N_BRANCH = 3
NORM_EPS = 1e-6
NEG_INF = -1e30
SPLIT_SIZES = (Q_LORA, KV_LORA, ROPE_DIM, MLA_WIDTH,
               CONV_DIM, CONV_DIM, CONV_DIM, CONV_DIM,
               X_WIDTH, X_WIDTH, N_BRANCH * D_MODEL)
SPLIT_POINTS = tuple(int(v) for v in np.cumsum(SPLIT_SIZES)[:-1])

LANES = 128
ROPE_LANES = MLA_HEADS * ROPE_DIM
HEAD_LANES = LANES
PACKED = MLA_HEADS * HEAD_LANES
KT_TILE = 256
N_PAGE_SLOTS = 3
VMEM_LIMIT = 48 * 1024 * 1024
BF16 = jnp.bfloat16
F32 = jnp.float32
NT_DIMS = (((1,), (1,)), ((), ()))
MLA_EXP2_SCALE = MLA_SCALE * math.log2(math.e)


def _dot(a, b):
    return jnp.dot(a, b, preferred_element_type=F32)


def _dot_nt(a, b):
    return lax.dot_general(a, b, NT_DIMS, preferred_element_type=F32)


def _rms(x, g):
    return x * lax.rsqrt(jnp.mean(x * x, axis=-1, keepdims=True) + NORM_EPS) * g


def _const_spec(shape):
    return pl.BlockSpec(shape, lambda *_: (0,) * len(shape), pipeline_mode=pl.Buffered(1))


def _mod_pow2(x, n):
    assert n & (n - 1) == 0
    return x & (n - 1)


def _params(n_axes):
    return pltpu.CompilerParams(dimension_semantics=("arbitrary",) * n_axes,
                                vmem_limit_bytes=VMEM_LIMIT)


def _qkv_prompt_kernel(x_ref, wa_ref, gq_ref, gkv_ref, wuq_ref, wukt_ref, wuv_ref,
                       cq_ref, sq_ref, ck_ref, sk_ref,
                       qp_ref, kt_ref, v_ref, ckv_ref, kr_ref):
    xb = x_ref[...].astype(BF16)
    pa = _dot(xb, wa_ref[...])
    cqn = _rms(pa[:, :Q_LORA], gq_ref[...])
    qa = _dot(cqn.astype(BF16), wuq_ref[...])
    cq = cq_ref[...]
    sq = sq_ref[...]
    for h in range(MLA_HEADS):
        lo = h * HEAD_LANES
        qp_ref[h] = (qa[:, lo:lo + HEAD_LANES] * cq
                     + qa[:, PACKED + lo:PACKED + lo + HEAD_LANES] * sq).astype(BF16)
    ckvn = _rms(pa[:, Q_LORA:Q_LORA + KV_LORA], gkv_ref[...])
    ckv_ref[...] = ckvn
    vp = _dot(ckvn.astype(BF16), wuv_ref[...])
    for h in range(MLA_HEADS):
        v_ref[h] = vp[:, h * HEAD_LANES:(h + 1) * HEAD_LANES].astype(BF16)
    k0 = Q_LORA + KV_LORA
    krot = pa[:, k0:k0 + HEAD_LANES] * ck_ref[...] + pa[:, k0 + HEAD_LANES:] * sk_ref[...]
    kr_ref[...] = krot[:, :ROPE_DIM]
    lane = lax.broadcasted_iota(jnp.int32, krot.shape, 1)
    k_rope = jnp.where(lane >= NOPE_DIM, krot, 0.0)
    for t in range(kt_ref.shape[0]):
        rows = slice(t * KT_TILE, (t + 1) * KT_TILE)
        k_nope_t = _dot(wukt_ref[...], ckvn[rows, :].T.astype(BF16))
        k_rope_t = k_rope[rows, :].T
        for h in range(MLA_HEADS):
            kt_ref[t, h] = (k_nope_t[h * HEAD_LANES:(h + 1) * HEAD_LANES, :] + k_rope_t).astype(BF16)


def _qkv_prompt(x2d, w, tables, tm, table_tiles):
    ntok = x2d.shape[0]
    tok = lambda width: pl.BlockSpec((tm, width), lambda t: (t, 0))
    tab = pl.BlockSpec((tm, HEAD_LANES), lambda t: (t % table_tiles, 0))
    head_tok = pl.BlockSpec((MLA_HEADS, tm, HEAD_LANES), lambda t: (0, t, 0))
    weights = [w["wa_p"], w["gq"], w["gkv"], w["wuq_p"], w["wukt_p"], w["wuv_p"]]
    return pl.pallas_call(
        _qkv_prompt_kernel,
        grid=(ntok // tm,),
        in_specs=[tok(D_MODEL)] + [_const_spec(a.shape) for a in weights] + [tab] * 4,
        out_specs=[head_tok,
                   pl.BlockSpec((tm // KT_TILE, MLA_HEADS, HEAD_LANES, KT_TILE), lambda t: (t, 0, 0, 0)),
                   head_tok, tok(KV_LORA), tok(ROPE_DIM)],
        out_shape=[jax.ShapeDtypeStruct((MLA_HEADS, ntok, HEAD_LANES), BF16),
                   jax.ShapeDtypeStruct((ntok // KT_TILE, MLA_HEADS, HEAD_LANES, KT_TILE), BF16),
                   jax.ShapeDtypeStruct((MLA_HEADS, ntok, HEAD_LANES), BF16),
                   jax.ShapeDtypeStruct((ntok, KV_LORA), F32),
                   jax.ShapeDtypeStruct((ntok, ROPE_DIM), F32)],
        compiler_params=_params(1),
        name="qkv_prompt",
    )(x2d, *weights, *tables)


def _qkv_sample_kernel(x_ref, wa_ref, gq_ref, gkv_ref, wuq_ref, wuk_ref, cos_ref, sin_ref,
                       qlat_ref, qrope_ref, ckv_ref, kr_ref):
    cos = cos_ref[...]
    sin = sin_ref[...]
    xb = x_ref[...].astype(BF16)
    pa = _dot(xb, wa_ref[...])
    cqn = _rms(pa[:, :Q_LORA], gq_ref[...])
    qa = _dot(cqn.astype(BF16), wuq_ref[...])
    r0 = MLA_HEADS * NOPE_DIM
    qrope_ref[...] = qa[:, r0:r0 + ROPE_LANES] * cos + qa[:, r0 + ROPE_LANES:] * sin
    for p in range(MLA_HEADS // 2):
        ql = _dot(qa[:, 128 * p:128 * (p + 1)].astype(BF16), wuk_ref[p])
        qlat_ref[2 * p] = ql[:, :KV_LORA]
        qlat_ref[2 * p + 1] = ql[:, KV_LORA:]
    ckv_ref[...] = _rms(pa[:, Q_LORA:Q_LORA + KV_LORA], gkv_ref[...])
    k0 = Q_LORA + KV_LORA
    krt = pa[:, k0:k0 + ROPE_LANES] * cos + pa[:, k0 + ROPE_LANES:] * sin
    kr_ref[...] = krt[:, :ROPE_DIM]


def _qkv_sample(x2d, w, cos, sin, tm):
    ntok = x2d.shape[0]
    tok = lambda width: pl.BlockSpec((tm, width), lambda t: (t, 0))
    tab = pl.BlockSpec((tm, ROPE_LANES), lambda t: (0, 0))
    weights = [w["wa"], w["gq"], w["gkv"], w["wuq"], w["wuk"]]
    return pl.pallas_call(
        _qkv_sample_kernel,
        grid=(ntok // tm,),
        in_specs=[tok(D_MODEL)] + [_const_spec(a.shape) for a in weights] + [tab, tab],
        out_specs=[pl.BlockSpec((MLA_HEADS, tm, KV_LORA), lambda t: (0, t, 0)),
                   tok(ROPE_LANES), tok(KV_LORA), tok(ROPE_DIM)],
        out_shape=[jax.ShapeDtypeStruct((MLA_HEADS, ntok, KV_LORA), F32),
                   jax.ShapeDtypeStruct((ntok, ROPE_LANES), F32),
                   jax.ShapeDtypeStruct((ntok, KV_LORA), F32),
                   jax.ShapeDtypeStruct((ntok, ROPE_DIM), F32)],
        compiler_params=_params(1),
        name="qkv_sample",
    )(x2d, *weights, cos, sin)


def _lane_repeat(x, width):
    return jnp.concatenate([x] * (width // LANES), axis=-1)


def _softmax_weights(s, m_old, l_old, scale):
    mul = (lambda x: x) if scale == 1.0 else (lambda x: x * scale)
    m_new = jnp.maximum(m_old, jnp.max(s, axis=-1, keepdims=True))
    alpha = jnp.exp2(mul(m_old - m_new))
    p = jnp.exp2(mul(s - _lane_repeat(m_new, s.shape[-1])))
    l_new = alpha * l_old + jnp.sum(p, axis=-1, keepdims=True)
    return m_new, l_new, alpha, p.astype(BF16)


def _softmax_init(m_scr, l_scr, acc_scr):
    m_scr[...] = jnp.full(m_scr.shape, -jnp.inf, F32)
    l_scr[...] = jnp.zeros(l_scr.shape, F32)
    acc_scr[...] = jnp.zeros(acc_scr.shape, F32)


def _mla_prompt_kernel(qp_ref, kt_ref, v_ref, y_ref, m_scr, l_scr, acc_scr, *, tq, rq):
    assert tq == KT_TILE and tq % rq == 0
    i = pl.program_id(1)
    _softmax_init(m_scr, l_scr, acc_scr)
    chunks = [(h, r0) for h in range(MLA_HEADS) for r0 in range(0, tq, rq)]
    first_of_pair = lax.broadcasted_iota(jnp.int32, (rq, HEAD_LANES), 1) < V_DIM

    def step(j, diagonal):
        keys = pl.ds(pl.multiple_of(j * tq, tq), tq)
        scores = [_dot(qp_ref[h, r0:r0 + rq, :], kt_ref[j, h]) for h, r0 in chunks]
        weights = {}
        for (h, r0), s in zip(chunks, scores):
            rs = slice(r0, r0 + rq)
            if diagonal:
                q_pos = r0 + lax.broadcasted_iota(jnp.int32, (rq, tq), 0)
                k_pos = lax.broadcasted_iota(jnp.int32, (rq, tq), 1)
                s = jnp.where(k_pos <= q_pos, s, NEG_INF)
            m_scr[h, rs, :], l_scr[h, rs, :], alpha, p = _softmax_weights(
                s, m_scr[h, rs, :], l_scr[h, rs, :], 1.0)
            weights[h, r0] = (alpha, p)
        for pair in range(MLA_HEADS // 2):
            ha, hb = 2 * pair, 2 * pair + 1
            for r0 in range(0, tq, rq):
                rs = slice(r0, r0 + rq)
                (alpha_a, p_a), (alpha_b, p_b) = weights[ha, r0], weights[hb, r0]
                pv = _dot(p_a, v_ref[ha, keys, :]) + _dot(p_b, v_ref[hb, keys, :])
                acc_scr[pair, rs, :] = (jnp.where(first_of_pair, alpha_a, alpha_b)
                                        * acc_scr[pair, rs, :] + pv)

    def body(j, carry):
        step(j, False)
        return carry

    lax.fori_loop(0, i, body, 0)
    step(i, True)
    first = lax.broadcasted_iota(jnp.int32, (tq, HEAD_LANES), 1) < V_DIM
    for pair in range(MLA_HEADS // 2):
        inv_l = jnp.where(first, 1.0 / l_scr[2 * pair], 1.0 / l_scr[2 * pair + 1])
        y_ref[:, pair * HEAD_LANES:(pair + 1) * HEAD_LANES] = acc_scr[pair] * inv_l


def _mla_prompt(qp, kt, v, batch, seq, tq):
    ntok = batch * seq
    nq = seq // tq
    return pl.pallas_call(
        functools.partial(_mla_prompt_kernel, tq=tq, rq=128),
        grid=(batch, nq),
        in_specs=[pl.BlockSpec((MLA_HEADS, tq, HEAD_LANES), lambda b, i: (0, b * nq + i, 0)),
                  pl.BlockSpec((seq // KT_TILE, MLA_HEADS, HEAD_LANES, KT_TILE),
                               lambda b, i: (b, 0, 0, 0)),
                  pl.BlockSpec((MLA_HEADS, seq, HEAD_LANES), lambda b, i: (0, b, 0))],
        out_specs=pl.BlockSpec((tq, MLA_WIDTH), lambda b, i: (b * nq + i, 0)),
        out_shape=jax.ShapeDtypeStruct((ntok, MLA_WIDTH), F32),
        scratch_shapes=[pltpu.VMEM((MLA_HEADS, tq, LANES), F32),
                        pltpu.VMEM((MLA_HEADS, tq, LANES), F32),
                        pltpu.VMEM((MLA_HEADS // 2, tq, HEAD_LANES), F32)],
        compiler_params=_params(2),
        name="mla_prompt",
    )(qp, kt, v)


def _mla_decode_kernel(pt_ref, qlat_ref, qrope_ref, ckvn_ref, krn_ref, ckv_hbm, krt_hbm, o_ref,
                       ckv_buf, krt_buf, sem, ql_scr, qr_scr, kn_scr, krn_scr, s_scr,
                       m_scr, l_scr, acc_scr, *, cpg, n_chunks, dec_seq):
    seq = pl.program_id(0)
    n_seq = pl.num_programs(0)
    rows = MLA_HEADS * dec_seq
    first_slot = lax.rem(seq * n_chunks, N_PAGE_SLOTS)

    def slot_after(slot):
        return jnp.where(slot == N_PAGE_SLOTS - 1, 0, slot + 1)

    def slot_before(slot):
        return jnp.where(slot == 0, N_PAGE_SLOTS - 1, slot - 1)

    def page_copies(sq, chunk, slot):
        copies = []
        for k in range(cpg):
            page = pt_ref[sq, chunk * cpg + k]
            copies.append(pltpu.make_async_copy(ckv_hbm.at[page], ckv_buf.at[slot, k], sem.at[slot, 0]))
            copies.append(pltpu.make_async_copy(krt_hbm.at[page], krt_buf.at[slot, k], sem.at[slot, 1]))
        return copies

    def chunk_after(c):
        last_chunk = c + 1 == n_chunks
        more_seqs = seq + 1 < n_seq
        sq = jnp.where(last_chunk & more_seqs, seq + 1, seq)
        ch = jnp.where(last_chunk, jnp.where(more_seqs, 0, c), c + 1)
        return sq, ch

    def fetch_next_and_wait(c, slot):
        sq, ch = chunk_after(c)
        for cp in page_copies(sq, ch, slot_after(slot)):
            cp.start()
        for cp in page_copies(seq, c, slot):
            cp.wait()

    def score(slot):
        ql = ql_scr[...]
        qr = qr_scr[...]
        for k in range(cpg):
            s_scr[:, k * PAGE_SIZE:(k + 1) * PAGE_SIZE] = (
                _dot_nt(ql, ckv_buf[slot, k].astype(BF16)) + _dot(qr, krt_buf[slot, k].astype(BF16)))

    def softmax():
        m_scr[...], l_scr[...], alpha, p = _softmax_weights(
            s_scr[...], m_scr[...], l_scr[...], MLA_EXP2_SCALE)
        return alpha, p

    def weigh(slot, alpha, p):
        pv = _dot(p[:, :PAGE_SIZE], ckv_buf[slot, 0].astype(BF16))
        for k in range(1, cpg):
            pv = pv + _dot(p[:, PAGE_SIZE * k:PAGE_SIZE * (k + 1)], ckv_buf[slot, k].astype(BF16))
        acc_scr[...] = _lane_repeat(alpha, KV_LORA) * acc_scr[...] + pv

    @pl.when(seq == 0)
    def _():
        for cp in page_copies(0, 0, 0):
            cp.start()

    ql_scr[...] = jnp.concatenate([qlat_ref[h] for h in range(MLA_HEADS)], axis=0).astype(BF16)
    qr = qrope_ref[...]
    qr_scr[...] = jnp.concatenate(
        [qr[:, ROPE_DIM * h:ROPE_DIM * (h + 1)] for h in range(MLA_HEADS)], axis=0).astype(BF16)
    kn_scr[...] = jnp.zeros(kn_scr.shape, F32)
    krn_scr[...] = jnp.zeros(krn_scr.shape, F32)
    kn_scr[:dec_seq, :] = ckvn_ref[...]
    krn_scr[:dec_seq, :] = krn_ref[...]
    _softmax_init(m_scr, l_scr, acc_scr)

    fetch_next_and_wait(0, first_slot)
    score(first_slot)

    def chunk_body(c, slot):
        fetch_next_and_wait(c, slot)
        alpha, p = softmax()
        score(slot)
        weigh(slot_before(slot), alpha, p)
        return slot_after(slot)

    last_slot = slot_before(lax.fori_loop(1, n_chunks, chunk_body, slot_after(first_slot)))
    alpha, p = softmax()
    weigh(last_slot, alpha, p)

    @pl.when(seq == n_seq - 1)
    def _():
        for cp in page_copies(seq, n_chunks - 1, slot_after(last_slot)):
            cp.wait()

    ck = kn_scr[...].astype(BF16)
    s_new = _dot_nt(ql_scr[...], ck) + _dot_nt(qr_scr[...], krn_scr[...].astype(BF16))
    q_pos = _mod_pow2(lax.broadcasted_iota(jnp.int32, (rows, PAGE_SIZE), 0), dec_seq)
    k_pos = lax.broadcasted_iota(jnp.int32, (rows, PAGE_SIZE), 1)
    s_new = jnp.where(k_pos <= q_pos, s_new, NEG_INF)
    _, l_fin, alpha, p = _softmax_weights(s_new, m_scr[...], l_scr[...], MLA_EXP2_SCALE)
    acc_fin = _lane_repeat(alpha, KV_LORA) * acc_scr[...] + _dot(p, ck)
    o = acc_fin * _lane_repeat(1.0 / l_fin, KV_LORA)
    for h in range(MLA_HEADS):
        o_ref[h] = o[h * dec_seq:(h + 1) * dec_seq, :]


def _mla_decode(page_table, qlat, qrope, ckv_new, kr_new, cache_ckv, cache_krt, dec_seq, cpg):
    n_seq, n_pages = page_table.shape
    ntok = n_seq * dec_seq
    rows = MLA_HEADS * dec_seq
    tok = lambda width: pl.BlockSpec((dec_seq, width), lambda s, pt: (s, 0))
    head_tok = pl.BlockSpec((MLA_HEADS, dec_seq, KV_LORA), lambda s, pt: (0, s, 0))
    hbm = pl.BlockSpec(memory_space=pl.ANY)
    grid_spec = pltpu.PrefetchScalarGridSpec(
        num_scalar_prefetch=1,
        grid=(n_seq,),
        in_specs=[head_tok, tok(ROPE_LANES), tok(KV_LORA), tok(ROPE_DIM), hbm, hbm],
        out_specs=head_tok,
        scratch_shapes=[pltpu.VMEM((N_PAGE_SLOTS, cpg, PAGE_SIZE, KV_LORA), F32),
                        pltpu.VMEM((N_PAGE_SLOTS, cpg, ROPE_DIM, PAGE_SIZE), F32),
                        pltpu.SemaphoreType.DMA((N_PAGE_SLOTS, 2)),
                        pltpu.VMEM((rows, KV_LORA), BF16), pltpu.VMEM((rows, ROPE_DIM), BF16),
                        pltpu.VMEM((PAGE_SIZE, KV_LORA), F32), pltpu.VMEM((PAGE_SIZE, ROPE_DIM), F32),
                        pltpu.VMEM((rows, cpg * PAGE_SIZE), F32),
                        pltpu.VMEM((rows, LANES), F32), pltpu.VMEM((rows, LANES), F32),
                        pltpu.VMEM((rows, KV_LORA), F32)],
    )
    return pl.pallas_call(
        functools.partial(_mla_decode_kernel, cpg=cpg, n_chunks=n_pages // cpg, dec_seq=dec_seq),
        grid_spec=grid_spec,
        out_shape=jax.ShapeDtypeStruct((MLA_HEADS, ntok, KV_LORA), F32),
        compiler_params=_params(1),
        name="mla_decode",
    )(page_table, qlat, qrope, ckv_new, kr_new, cache_ckv, cache_krt)


def _uv_kernel(o_ref, wuv_ref, y_ref):
    y = _dot(o_ref[0].astype(BF16), wuv_ref[0])
    for h in range(1, MLA_HEADS):
        y = y + _dot(o_ref[h].astype(BF16), wuv_ref[h])
    y_ref[...] = y


def _uv(o_lat, wuv, tm):
    ntok = o_lat.shape[1]
    return pl.pallas_call(
        _uv_kernel,
        grid=(ntok // tm,),
        in_specs=[pl.BlockSpec((MLA_HEADS, tm, KV_LORA), lambda t: (0, t, 0)), _const_spec(wuv.shape)],
        out_specs=pl.BlockSpec((tm, MLA_WIDTH), lambda t: (t, 0)),
        out_shape=jax.ShapeDtypeStruct((ntok, MLA_WIDTH), F32),
        compiler_params=_params(1),
        name="uv",
    )(o_lat, wuv)


def _memkv_kernel(mem_ref, wk_ref, wv_ref, k_ref, v_ref):
    mb = mem_ref[...].astype(BF16)
    k_ref[...] = _dot(mb, wk_ref[...])
    v_ref[...] = _dot(mb, wv_ref[...])


def _memkv(mem2d, wk, wv, tm):
    n = mem2d.shape[0]
    out = pl.BlockSpec((tm, X_WIDTH), lambda t: (t, 0))
    return pl.pallas_call(
        _memkv_kernel,
        grid=(n // tm,),
        in_specs=[pl.BlockSpec((tm, D_MODEL), lambda t: (t, 0)), _const_spec(wk.shape),
                  _const_spec(wv.shape)],
        out_specs=[out, out],
        out_shape=[jax.ShapeDtypeStruct((n, X_WIDTH), F32)] * 2,
        compiler_params=_params(1),
        name="memkv",
    )(mem2d, wk, wv)


def _xattn_kernel(x_ref, wqx_ref, mk_ref, mv_ref, y_ref, *, groups, g, head_interleaved):
    qx = _dot(x_ref[...].astype(BF16), wqx_ref[...])
    units = [(s, h) for s in range(groups) for h in range(X_HEADS)]
    rows = lambda s: slice(s * g, (s + 1) * g)
    cols = lambda h: slice(h * X_HEAD_DIM, (h + 1) * X_HEAD_DIM)

    def head_of(ref, s, h):
        if head_interleaved:
            return ref[s, pl.ds(h, MEM_LEN, stride=X_HEADS), :].astype(BF16)
        return ref[s, :, cols(h)].astype(BF16)

    scores = [_dot_nt(qx[rows(s), cols(h)].astype(BF16), head_of(mk_ref, s, h)) for s, h in units]
    weights = []
    for sc in scores:
        sc = sc * X_SCALE
        p = jnp.exp(sc - jnp.max(sc, axis=-1, keepdims=True))
        weights.append((p / jnp.sum(p, axis=-1, keepdims=True)).astype(BF16))
    for (s, h), p in zip(units, weights):
        y_ref[rows(s), cols(h)] = _dot(p, head_of(mv_ref, s, h))


def _xattn(x2d, wqx, mem_k, mem_v, groups, g, tiles_per_mem):
    ntok = x2d.shape[0]
    tm = groups * g
    head_interleaved = mem_k.shape[-1] == X_HEAD_DIM
    mem = pl.BlockSpec((groups,) + mem_k.shape[1:], lambda t: (t // tiles_per_mem, 0, 0))
    return pl.pallas_call(
        functools.partial(_xattn_kernel, groups=groups, g=g, head_interleaved=head_interleaved),
        grid=(ntok // tm,),
        in_specs=[pl.BlockSpec((tm, D_MODEL), lambda t: (t, 0)), _const_spec(wqx.shape), mem, mem],
        out_specs=pl.BlockSpec((tm, X_WIDTH), lambda t: (t, 0)),
        out_shape=jax.ShapeDtypeStruct((ntok, X_WIDTH), F32),
        compiler_params=_params(1),
        name="xattn",
    )(x2d, wqx, mem_k, mem_v)


def _merge_kernel(x_ref, ymla_ref, yx_ref, prev_ref, wb_ref, wpm_ref, wpc_ref, wpx_ref, wo_ref,
                  convw_ref, lng_ref, lnb_ref, out_ref, pre_ref, carry_scr,
                  *, tm, seq_rows, tiles_per_seq, alpha):
    x = x_ref[...]
    xb = x.astype(BF16)

    def proj(k):
        lo = k * CONV_DIM if k < 6 else 6 * CONV_DIM + (k - 6) * D_MODEL
        hi = lo + (CONV_DIM if k < 6 else D_MODEL)
        return _dot(xb, wb_ref[:, lo:hi])

    def branch(y, z, gate, wp_ref):
        return jax.nn.sigmoid(gate) * _dot((y * jax.nn.silu(z)).astype(BF16), wp_ref[...])

    m = branch(ymla_ref[...], proj(0), proj(6), wpm_ref)

    pre = proj(2) * proj(3)
    row = lax.broadcasted_iota(jnp.int32, (tm, CONV_DIM), 0)
    if tiles_per_seq is not None:
        @pl.when(pl.program_id(0) % tiles_per_seq == 0)
        def _():
            carry_scr[...] = jnp.zeros(carry_scr.shape, F32)
        prev1 = jnp.broadcast_to(carry_scr[7:8, :], (tm, CONV_DIM))
        prev0 = jnp.broadcast_to(carry_scr[6:7, :], (tm, CONV_DIM))
        pos = row
    else:
        n_seq = tm // seq_rows
        prev = prev_ref[...]
        bc = lambda r: jnp.broadcast_to(prev[:, r:r + 1, :], (n_seq, seq_rows, CONV_DIM)).reshape(tm, CONV_DIM)
        prev0, prev1 = bc(0), bc(1)
        pos = _mod_pow2(row, seq_rows)
    back1 = jnp.where(pos == 0, prev1, pltpu.roll(pre, 1, 0))
    back2 = jnp.where(pos == 0, prev0, jnp.where(pos == 1, prev1, pltpu.roll(pre, 2, 0)))
    cw = convw_ref[...]
    conv = cw[0:1, :] * back2 + cw[1:2, :] * back1 + cw[2:3, :] * pre
    pre_ref[...] = pre
    if tiles_per_seq is not None:
        carry_scr[...] = pre[tm - 8:, :]
    m = m + branch(proj(1) * conv, proj(4), proj(7), wpc_ref)

    m = m + branch(yx_ref[...], proj(5), proj(8), wpx_ref)

    r = alpha * x + _dot(m.astype(BF16), wo_ref[...])
    mu = jnp.mean(r, axis=-1, keepdims=True)
    d = r - mu
    var = jnp.mean(d * d, axis=-1, keepdims=True)
    out_ref[...] = d * lax.rsqrt(var + NORM_EPS) * lng_ref[...] + lnb_ref[...]


def _merge(x2d, ymla, yx, prev, w, tm, seq_rows, tiles_per_seq, alpha):
    ntok = x2d.shape[0]
    tok = lambda width: pl.BlockSpec((tm, width), lambda t: (t, 0))
    n_seq = max(tm // seq_rows, 1)
    prev_spec = pl.BlockSpec((n_seq, CONV_WIDTH - 1, CONV_DIM),
                             lambda t: (t if tiles_per_seq is None else 0, 0, 0))
    weights = [w["wb"], w["wpm"], w["wpc"], w["wpx"], w["wo"], w["convw"], w["lng"], w["lnb"]]
    return pl.pallas_call(
        functools.partial(_merge_kernel, tm=tm, seq_rows=seq_rows, tiles_per_seq=tiles_per_seq,
                          alpha=alpha),
        grid=(ntok // tm,),
        in_specs=[tok(D_MODEL), tok(MLA_WIDTH), tok(X_WIDTH), prev_spec]
        + [_const_spec(a.shape) for a in weights],
        out_specs=[tok(D_MODEL), tok(CONV_DIM)],
        out_shape=[jax.ShapeDtypeStruct((ntok, D_MODEL), F32),
                   jax.ShapeDtypeStruct((ntok, CONV_DIM), F32)],
        scratch_shapes=[pltpu.VMEM((8, CONV_DIM), F32)],
        compiler_params=_params(1),
        name="merge",
    )(x2d, ymla, yx, prev, *weights)


def _swap_halves(w):
    half = ROPE_DIM // 2
    return jnp.concatenate([w[..., half:], w[..., :half]], axis=-1)


def _place_rope(w):
    z = jnp.zeros_like(w)
    return jnp.concatenate([w, z, w, z], axis=-1)


def _pack_weights(w_in, q_norm_g, kv_norm_g, w_uq, w_uk, w_uv, conv_w, w_mk, w_mv,
                  w_p_mla, w_p_conv, w_p_x, w_o, ln_g, ln_b):
    (w_cq, w_ckv, w_kr, w_zmla, w_cb, w_cc, w_ch, w_zc, w_qx, w_zx, w_g) = jnp.split(
        w_in, SPLIT_POINTS, axis=-1)
    tile_heads = lambda a: jnp.tile(a, (1, MLA_HEADS))
    uq = w_uq.reshape(Q_LORA, MLA_HEADS, NOPE_DIM + ROPE_DIM)
    uq_nope, uq_rope = uq[:, :, :NOPE_DIM], uq[:, :, NOPE_DIM:]
    uk_t = jnp.transpose(w_uk, (1, 2, 0))
    uv_t = jnp.transpose(w_uv, (1, 0, 2))
    wa = jnp.concatenate([w_cq, w_ckv, tile_heads(w_kr), tile_heads(_swap_halves(w_kr))], axis=1)
    wuq = jnp.concatenate([uq_nope.reshape(Q_LORA, -1), uq_rope.reshape(Q_LORA, -1),
                           _swap_halves(uq_rope).reshape(Q_LORA, -1)], axis=1)
    zeros = jnp.zeros((MLA_HEADS // 2, NOPE_DIM, KV_LORA), w_uk.dtype)
    wuk = jnp.concatenate([jnp.concatenate([uk_t[0::2], zeros], axis=2),
                           jnp.concatenate([zeros, uk_t[1::2]], axis=2)], axis=1)
    wuv = jnp.stack([jnp.pad(uv_t[h], ((0, 0), (h * V_DIM, MLA_WIDTH - (h + 1) * V_DIM)))
                     for h in range(MLA_HEADS)])
    wa_p = jnp.concatenate([w_cq, w_ckv, _place_rope(w_kr), _place_rope(_swap_halves(w_kr))], axis=1)
    pad_q = jnp.zeros((Q_LORA, MLA_HEADS, HEAD_LANES - NOPE_DIM - ROPE_DIM), w_uq.dtype)
    wuq_p = jnp.concatenate([
        jnp.concatenate([uq_nope, uq_rope, pad_q], axis=-1).reshape(Q_LORA, PACKED),
        jnp.concatenate([jnp.zeros_like(uq_nope), _swap_halves(uq_rope), pad_q],
                        axis=-1).reshape(Q_LORA, PACKED)], axis=1)
    wukt_p = jnp.concatenate([uk_t, jnp.zeros((MLA_HEADS, HEAD_LANES - NOPE_DIM, KV_LORA), w_uk.dtype)],
                             axis=1).reshape(PACKED, KV_LORA)
    wuv_p = jnp.concatenate(
        [jnp.pad(uv_t[h], ((0, 0), ((h % 2) * V_DIM, HEAD_LANES - (h % 2 + 1) * V_DIM)))
         for h in range(MLA_HEADS)], axis=1)
    wb = jnp.concatenate([w_zmla, w_cb, w_cc, w_ch, w_zc, w_zx, w_g], axis=1)
    bf = lambda a: a.astype(BF16)
    row = lambda a: a.reshape(1, -1).astype(F32)
    return dict(wa=bf(wa), gq=row(q_norm_g), gkv=row(kv_norm_g), wuq=bf(wuq), wuk=bf(wuk),
                wuv=bf(wuv), wa_p=bf(wa_p), wuq_p=bf(wuq_p), wukt_p=bf(wukt_p), wuv_p=bf(wuv_p),
                wqx=bf(w_qx), wmk=bf(w_mk), wmv=bf(w_mv), wb=bf(wb),
                wpm=bf(w_p_mla), wpc=bf(w_p_conv), wpx=bf(w_p_x), wo=bf(w_o),
                convw=conv_w.astype(F32), lng=row(ln_g), lnb=row(ln_b))


def _rope_cos_sin(pos):
    half = ROPE_DIM // 2
    inv_freq = ROPE_THETA ** (-jnp.arange(half, dtype=jnp.float32) * (2.0 / ROPE_DIM))
    ang = pos.astype(jnp.float32)[:, None] * inv_freq[None, :]
    cos, sin = jnp.cos(ang), jnp.sin(ang)
    return jnp.concatenate([cos, cos], axis=-1), jnp.concatenate([-sin, sin], axis=-1)


def _prompt_tables(pos):
    cos, sin = _rope_cos_sin(pos)
    n = pos.shape[0]
    ones = jnp.ones((n, NOPE_DIM), F32)
    z32 = jnp.zeros((n, ROPE_DIM), F32)
    z64 = jnp.zeros((n, NOPE_DIM), F32)
    cq = jnp.concatenate([ones, cos, z32], axis=-1) * MLA_EXP2_SCALE
    sq = jnp.concatenate([z64, sin, z32], axis=-1) * MLA_EXP2_SCALE
    return cq, sq, _place_rope(cos), _place_rope(sin)


def kernel(x_prompt, x_sample, mem_prompt, cache_ckv, cache_kr, cache_conv, cache_mem_k,
           cache_mem_v, page_table, w_in, q_norm_g, kv_norm_g, w_uq, w_uk, w_uv, conv_w,
           w_mk, w_mv, w_p_mla, w_p_conv, w_p_x, w_o, ln_g, ln_b):
    depth = w_in.shape[0]
    batch, seq, _ = x_prompt.shape
    dec_batch, dec_seq, _ = x_sample.shape
    past_len = page_table.shape[1] * PAGE_SIZE
    alpha = (2 * depth) ** 0.25
    tm_p, tq, tm_s, x_groups_s, cpg = 512, KT_TILE, 256, 8, 16

    tables_p = _prompt_tables(jnp.arange(seq, dtype=jnp.float32))
    cos_s, sin_s = _rope_cos_sin(past_len + jnp.arange(dec_seq, dtype=jnp.float32))
    cos_s = jnp.tile(cos_s, (tm_s // dec_seq, MLA_HEADS))
    sin_s = jnp.tile(sin_s, (tm_s // dec_seq, MLA_HEADS))

    hp = x_prompt.reshape(batch * seq, D_MODEL)
    hs = x_sample.reshape(dec_batch * dec_seq, D_MODEL)
    mem_shape = (MEM_LEN, X_HEADS, X_HEAD_DIM)
    outs = [[] for _ in range(8)]
    for l in range(depth):
        w = _pack_weights(w_in[l], q_norm_g[l], kv_norm_g[l], w_uq[l], w_uk[l], w_uv[l], conv_w[l],
                          w_mk[l], w_mv[l], w_p_mla[l], w_p_conv[l], w_p_x[l], w_o[l], ln_g[l], ln_b[l])
        qp, kt, v, ckv_p, kr_p = _qkv_prompt(hp, w, tables_p, tm_p, seq // tm_p)
        ymla = _mla_prompt(qp, kt, v, batch, seq, tq)
        memk, memv = _memkv(mem_prompt.reshape(batch * MEM_LEN, D_MODEL), w["wmk"], w["wmv"], MEM_LEN)
        yx = _xattn(hp, w["wqx"], memk.reshape(batch, MEM_LEN, X_WIDTH),
                    memv.reshape(batch, MEM_LEN, X_WIDTH), 1, tm_p, seq // tm_p)
        memk = memk.reshape(batch, *mem_shape)
        memv = memv.reshape(batch, *mem_shape)
        no_prev = jnp.zeros((1, CONV_WIDTH - 1, CONV_DIM), F32)
        hp, pre_p = _merge(hp, ymla, yx, no_prev, w, tm_p, seq, seq // tm_p, alpha)
        qlat, qrope, ckv_s, kr_s = _qkv_sample(hs, w, cos_s, sin_s, tm_s)
        o_lat = _mla_decode(page_table, qlat, qrope, ckv_s, kr_s, cache_ckv[l],
                            jnp.swapaxes(cache_kr[l], 1, 2), dec_seq, cpg)
        ymla = _uv(o_lat, w["wuv"], tm_s)
        interleaved = (dec_batch, MEM_LEN * X_HEADS, X_HEAD_DIM)
        yx = _xattn(hs, w["wqx"], cache_mem_k[l].reshape(interleaved),
                    cache_mem_v[l].reshape(interleaved), x_groups_s, dec_seq, 1)
        hs, pre_s = _merge(hs, ymla, yx, cache_conv[l], w, tm_s, dec_seq, None, alpha)

        keep = CONV_WIDTH - 1
        for lst, val in zip(outs, (
                ckv_p.reshape(batch, seq, KV_LORA), kr_p.reshape(batch, seq, ROPE_DIM),
                pre_p.reshape(batch, seq, CONV_DIM)[:, seq - keep:],
                memk, memv,
                ckv_s.reshape(dec_batch, dec_seq, KV_LORA), kr_s.reshape(dec_batch, dec_seq, ROPE_DIM),
                pre_s.reshape(dec_batch, dec_seq, CONV_DIM)[:, dec_seq - keep:])):
            lst.append(val)
    return (hp.reshape(batch, seq, D_MODEL), hs.reshape(dec_batch, dec_seq, D_MODEL),
            *[jnp.stack(lst) for lst in outs])
```

```python
import functools
import math

import jax
import jax.numpy as jnp
import numpy as np
from jax import lax
from jax.experimental import pallas as pl
from jax.experimental.pallas import tpu as pltpu

D_MODEL = 1024
PAGE_SIZE = 128
MLA_HEADS = 8
NOPE_DIM = 64
ROPE_DIM = 32
V_DIM = 64
Q_LORA = 384
KV_LORA = 256
MLA_WIDTH = MLA_HEADS * V_DIM
MLA_SCALE = (NOPE_DIM + ROPE_DIM) ** -0.5
ROPE_THETA = 10000.0
CONV_DIM = D_MODEL // 2
CONV_WIDTH = 3
MEM_LEN = 256
X_HEADS = 4
X_HEAD_DIM = 128
X_WIDTH = X_HEADS * X_HEAD_DIM
X_SCALE = X_HEAD_DIM ** -0.5
N_BRANCH = 3
NORM_EPS = 1e-6
NEG_INF = -1e30
SPLIT_SIZES = (Q_LORA, KV_LORA, ROPE_DIM, MLA_WIDTH,
               CONV_DIM, CONV_DIM, CONV_DIM, CONV_DIM,
               X_WIDTH, X_WIDTH, N_BRANCH * D_MODEL)
SPLIT_POINTS = tuple(int(v) for v in np.cumsum(SPLIT_SIZES)[:-1])

LANES = 128
ROPE_LANES = MLA_HEADS * ROPE_DIM
HEAD_LANES = LANES
PACKED = MLA_HEADS * HEAD_LANES
KT_TILE = 256
FETCH_AHEAD = 2
N_PAGE_SLOTS = FETCH_AHEAD + 2
VMEM_LIMIT = 48 * 1024 * 1024
BF16 = jnp.bfloat16
F32 = jnp.float32
NT_DIMS = (((1,), (1,)), ((), ()))
MLA_EXP2_SCALE = MLA_SCALE * math.log2(math.e)


def _dot(a, b):
    return jnp.dot(a, b, preferred_element_type=F32)


def _dot_nt(a, b):
    return lax.dot_general(a, b, NT_DIMS, preferred_element_type=F32)


def _rms(x, g):
    return x * lax.rsqrt(jnp.mean(x * x, axis=-1, keepdims=True) + NORM_EPS) * g


def _const_spec(shape):
    return pl.BlockSpec(shape, lambda *_: (0,) * len(shape), pipeline_mode=pl.Buffered(1))


def _mod_pow2(x, n):
    assert n & (n - 1) == 0
    return x & (n - 1)


def _params(n_axes):
    return pltpu.CompilerParams(dimension_semantics=("arbitrary",) * n_axes,
                                vmem_limit_bytes=VMEM_LIMIT)


def _qkv_prompt_kernel(x_ref, wa_ref, gq_ref, gkv_ref, wuq_ref, wukt_ref, wuv_ref,
                       cq_ref, sq_ref, ck_ref, sk_ref,
                       qp_ref, kt_ref, v_ref, ckv_ref, kr_ref):
    xb = x_ref[...].astype(BF16)
    pa = _dot(xb, wa_ref[...])
    cqn = _rms(pa[:, :Q_LORA], gq_ref[...])
    qa = _dot(cqn.astype(BF16), wuq_ref[...])
    cq = cq_ref[...]
    sq = sq_ref[...]
    for h in range(MLA_HEADS):
        lo = h * HEAD_LANES
        qp_ref[h] = (qa[:, lo:lo + HEAD_LANES] * cq
                     + qa[:, PACKED + lo:PACKED + lo + HEAD_LANES] * sq).astype(BF16)
    ckvn = _rms(pa[:, Q_LORA:Q_LORA + KV_LORA], gkv_ref[...])
    ckv_ref[...] = ckvn
    vp = _dot(ckvn.astype(BF16), wuv_ref[...])
    for h in range(MLA_HEADS):
        v_ref[h] = vp[:, h * HEAD_LANES:(h + 1) * HEAD_LANES].astype(BF16)
    k0 = Q_LORA + KV_LORA
    krot = pa[:, k0:k0 + HEAD_LANES] * ck_ref[...] + pa[:, k0 + HEAD_LANES:] * sk_ref[...]
    kr_ref[...] = krot[:, :ROPE_DIM]
    lane = lax.broadcasted_iota(jnp.int32, krot.shape, 1)
    k_rope = jnp.where(lane >= NOPE_DIM, krot, 0.0)
    for t in range(kt_ref.shape[0]):
        rows = slice(t * KT_TILE, (t + 1) * KT_TILE)
        k_nope_t = _dot(wukt_ref[...], ckvn[rows, :].T.astype(BF16))
        k_rope_t = k_rope[rows, :].T
        for h in range(MLA_HEADS):
            kt_ref[t, h] = (k_nope_t[h * HEAD_LANES:(h + 1) * HEAD_LANES, :] + k_rope_t).astype(BF16)


def _qkv_prompt(x2d, w, tables, tm, table_tiles):
    ntok = x2d.shape[0]
    tok = lambda width: pl.BlockSpec((tm, width), lambda t: (t, 0))
    tab = pl.BlockSpec((tm, HEAD_LANES), lambda t: (t % table_tiles, 0))
    head_tok = pl.BlockSpec((MLA_HEADS, tm, HEAD_LANES), lambda t: (0, t, 0))
    weights = [w["wa_p"], w["gq"], w["gkv"], w["wuq_p"], w["wukt_p"], w["wuv_p"]]
    return pl.pallas_call(
        _qkv_prompt_kernel,
        grid=(ntok // tm,),
        in_specs=[tok(D_MODEL)] + [_const_spec(a.shape) for a in weights] + [tab] * 4,
        out_specs=[head_tok,
                   pl.BlockSpec((tm // KT_TILE, MLA_HEADS, HEAD_LANES, KT_TILE), lambda t: (t, 0, 0, 0)),
                   head_tok, tok(KV_LORA), tok(ROPE_DIM)],
        out_shape=[jax.ShapeDtypeStruct((MLA_HEADS, ntok, HEAD_LANES), BF16),
                   jax.ShapeDtypeStruct((ntok // KT_TILE, MLA_HEADS, HEAD_LANES, KT_TILE), BF16),
                   jax.ShapeDtypeStruct((MLA_HEADS, ntok, HEAD_LANES), BF16),
                   jax.ShapeDtypeStruct((ntok, KV_LORA), F32),
                   jax.ShapeDtypeStruct((ntok, ROPE_DIM), F32)],
        compiler_params=_params(1),
        name="qkv_prompt",
    )(x2d, *weights, *tables)


def _qkv_sample_kernel(x_ref, wa_ref, gq_ref, gkv_ref, wuq_ref, wuk_ref, cos_ref, sin_ref,
                       qlat_ref, qrope_ref, ckv_ref, kr_ref):
    cos = cos_ref[...]
    sin = sin_ref[...]
    xb = x_ref[...].astype(BF16)
    pa = _dot(xb, wa_ref[...])
    cqn = _rms(pa[:, :Q_LORA], gq_ref[...])
    qa = _dot(cqn.astype(BF16), wuq_ref[...])
    r0 = MLA_HEADS * NOPE_DIM
    qrope_ref[...] = qa[:, r0:r0 + ROPE_LANES] * cos + qa[:, r0 + ROPE_LANES:] * sin
    for p in range(MLA_HEADS // 2):
        ql = _dot(qa[:, 128 * p:128 * (p + 1)].astype(BF16), wuk_ref[p])
        qlat_ref[2 * p] = ql[:, :KV_LORA]
        qlat_ref[2 * p + 1] = ql[:, KV_LORA:]
    ckv_ref[...] = _rms(pa[:, Q_LORA:Q_LORA + KV_LORA], gkv_ref[...])
    k0 = Q_LORA + KV_LORA
    krt = pa[:, k0:k0 + ROPE_LANES] * cos + pa[:, k0 + ROPE_LANES:] * sin
    kr_ref[...] = krt[:, :ROPE_DIM]


def _qkv_sample(x2d, w, cos, sin, tm):
    ntok = x2d.shape[0]
    tok = lambda width: pl.BlockSpec((tm, width), lambda t: (t, 0))
    tab = pl.BlockSpec((tm, ROPE_LANES), lambda t: (0, 0))
    weights = [w["wa"], w["gq"], w["gkv"], w["wuq"], w["wuk"]]
    return pl.pallas_call(
        _qkv_sample_kernel,
        grid=(ntok // tm,),
        in_specs=[tok(D_MODEL)] + [_const_spec(a.shape) for a in weights] + [tab, tab],
        out_specs=[pl.BlockSpec((MLA_HEADS, tm, KV_LORA), lambda t: (0, t, 0)),
                   tok(ROPE_LANES), tok(KV_LORA), tok(ROPE_DIM)],
        out_shape=[jax.ShapeDtypeStruct((MLA_HEADS, ntok, KV_LORA), F32),
                   jax.ShapeDtypeStruct((ntok, ROPE_LANES), F32),
                   jax.ShapeDtypeStruct((ntok, KV_LORA), F32),
                   jax.ShapeDtypeStruct((ntok, ROPE_DIM), F32)],
        compiler_params=_params(1),
        name="qkv_sample",
    )(x2d, *weights, cos, sin)


def _lane_repeat(x, width):
    return jnp.concatenate([x] * (width // LANES), axis=-1)


def _softmax_weights(s, m_old, l_old, scale):
    mul = (lambda x: x) if scale == 1.0 else (lambda x: x * scale)
    m_new = jnp.maximum(m_old, jnp.max(s, axis=-1, keepdims=True))
    alpha = jnp.exp2(mul(m_old - m_new))
    p = jnp.exp2(mul(s - _lane_repeat(m_new, s.shape[-1])))
    l_new = alpha * l_old + jnp.sum(p, axis=-1, keepdims=True)
    return m_new, l_new, alpha, p.astype(BF16)


def _softmax_init(m_scr, l_scr, acc_scr):
    m_scr[...] = jnp.full(m_scr.shape, -jnp.inf, F32)
    l_scr[...] = jnp.zeros(l_scr.shape, F32)
    acc_scr[...] = jnp.zeros(acc_scr.shape, F32)


def _mla_prompt_kernel(qp_ref, kt_ref, v_ref, y_ref, m_scr, l_scr, acc_scr, *, tq, rq):
    assert tq == KT_TILE and tq % rq == 0
    i = pl.program_id(1)
    _softmax_init(m_scr, l_scr, acc_scr)
    chunks = [(h, r0) for h in range(MLA_HEADS) for r0 in range(0, tq, rq)]
    first_of_pair = lax.broadcasted_iota(jnp.int32, (rq, HEAD_LANES), 1) < V_DIM

    def step(j, n_tiles, diagonal):
        keys = pl.ds(pl.multiple_of(j * tq, tq), n_tiles * tq)
        scores = [jnp.concatenate([_dot(qp_ref[h, r0:r0 + rq, :], kt_ref[j + t, h])
                                   for t in range(n_tiles)], axis=1) for h, r0 in chunks]
        weights = {}
        for (h, r0), s in zip(chunks, scores):
            rs = slice(r0, r0 + rq)
            if diagonal:
                assert n_tiles == 1
                q_pos = r0 + lax.broadcasted_iota(jnp.int32, (rq, tq), 0)
                k_pos = lax.broadcasted_iota(jnp.int32, (rq, tq), 1)
                s = jnp.where(k_pos <= q_pos, s, NEG_INF)
            m_scr[h, rs, :], l_scr[h, rs, :], alpha, p = _softmax_weights(
                s, m_scr[h, rs, :], l_scr[h, rs, :], 1.0)
            weights[h, r0] = (alpha, p)
        for pair in range(MLA_HEADS // 2):
            ha, hb = 2 * pair, 2 * pair + 1
            for r0 in range(0, tq, rq):
                rs = slice(r0, r0 + rq)
                (alpha_a, p_a), (alpha_b, p_b) = weights[ha, r0], weights[hb, r0]
                pv = _dot(p_a, v_ref[ha, keys, :]) + _dot(p_b, v_ref[hb, keys, :])
                acc_scr[pair, rs, :] = (jnp.where(first_of_pair, alpha_a, alpha_b)
                                        * acc_scr[pair, rs, :] + pv)

    def body(j2, carry):
        step(2 * j2, 2, False)
        return carry

    lax.fori_loop(0, lax.shift_right_logical(i, 1), body, 0)

    @pl.when((i & 1) == 1)
    def _():
        step(i - 1, 1, False)

    step(i, 1, True)
    first = lax.broadcasted_iota(jnp.int32, (tq, HEAD_LANES), 1) < V_DIM
    for pair in range(MLA_HEADS // 2):
        inv_l = jnp.where(first, 1.0 / l_scr[2 * pair], 1.0 / l_scr[2 * pair + 1])
        y_ref[:, pair * HEAD_LANES:(pair + 1) * HEAD_LANES] = acc_scr[pair] * inv_l


def _mla_prompt(qp, kt, v, batch, seq, tq):
    ntok = batch * seq
    nq = seq // tq
    return pl.pallas_call(
        functools.partial(_mla_prompt_kernel, tq=tq, rq=128),
        grid=(batch, nq),
        in_specs=[pl.BlockSpec((MLA_HEADS, tq, HEAD_LANES), lambda b, i: (0, b * nq + i, 0)),
                  pl.BlockSpec((seq // KT_TILE, MLA_HEADS, HEAD_LANES, KT_TILE),
                               lambda b, i: (b, 0, 0, 0)),
                  pl.BlockSpec((MLA_HEADS, seq, HEAD_LANES), lambda b, i: (0, b, 0))],
        out_specs=pl.BlockSpec((tq, MLA_WIDTH), lambda b, i: (b * nq + i, 0)),
        out_shape=jax.ShapeDtypeStruct((ntok, MLA_WIDTH), F32),
        scratch_shapes=[pltpu.VMEM((MLA_HEADS, tq, LANES), F32),
                        pltpu.VMEM((MLA_HEADS, tq, LANES), F32),
                        pltpu.VMEM((MLA_HEADS // 2, tq, HEAD_LANES), F32)],
        compiler_params=_params(2),
        name="mla_prompt",
    )(qp, kt, v)


def _mla_decode_kernel(pt_ref, qlat_ref, qrope_ref, ckvn_ref, krn_ref, ckv_hbm, krt_hbm, o_ref,
                       ckv_buf, krt_buf, sem, ql_scr, qr_scr, kn_scr, krn_scr, s_scr,
                       m_scr, l_scr, acc_scr, *, cpg, n_chunks, dec_seq):
    assert n_chunks % N_PAGE_SLOTS == 0 and N_PAGE_SLOTS == FETCH_AHEAD + 2
    seq = pl.program_id(0)
    n_seq = pl.num_programs(0)
    rows = MLA_HEADS * dec_seq

    def slot_of(c):
        return _mod_pow2(c, N_PAGE_SLOTS)

    def page_copies(sq, chunk, slot):
        copies = []
        for k in range(cpg):
            page = pt_ref[sq, chunk * cpg + k]
            copies.append(pltpu.make_async_copy(ckv_hbm.at[page], ckv_buf.at[slot, k], sem.at[slot, 0]))
            copies.append(pltpu.make_async_copy(krt_hbm.at[page], krt_buf.at[slot, k], sem.at[slot, 1]))
        return copies

    def start_chunk(sq, chunk, slot):
        for n, cp in enumerate(page_copies(sq, chunk, slot)):
            cp.start(priority=(n // 2) % 2)

    def chunk_ahead(c):
        ahead = c + FETCH_AHEAD
        spills = ahead >= n_chunks
        more_seqs = seq + 1 < n_seq
        sq = jnp.where(spills & more_seqs, seq + 1, seq)
        ch = jnp.where(spills, jnp.where(more_seqs, ahead - n_chunks, c), ahead)
        return sq, ch

    def fetch_ahead_and_wait(c):
        start_chunk(*chunk_ahead(c), slot_of(c + FETCH_AHEAD))
        for cp in page_copies(seq, c, slot_of(c)):
            cp.wait()

    def score(slot):
        ql = ql_scr[...]
        qr = qr_scr[...]
        for k in range(cpg):
            s_scr[:, k * PAGE_SIZE:(k + 1) * PAGE_SIZE] = (
                _dot_nt(ql, ckv_buf[slot, k].astype(BF16)) + _dot(qr, krt_buf[slot, k].astype(BF16)))

    def softmax():
        m_scr[...], l_scr[...], alpha, p = _softmax_weights(
            s_scr[...], m_scr[...], l_scr[...], MLA_EXP2_SCALE)
        return alpha, p

    def weigh(slot, alpha, p):
        pv = _dot(p[:, :PAGE_SIZE], ckv_buf[slot, 0].astype(BF16))
        for k in range(1, cpg):
            pv = pv + _dot(p[:, PAGE_SIZE * k:PAGE_SIZE * (k + 1)], ckv_buf[slot, k].astype(BF16))
        acc_scr[...] = _lane_repeat(alpha, KV_LORA) * acc_scr[...] + pv

    @pl.when(seq == 0)
    def _():
        for c in range(FETCH_AHEAD):
            start_chunk(0, c, c)

    ql_scr[...] = jnp.concatenate([qlat_ref[h] for h in range(MLA_HEADS)], axis=0).astype(BF16)
    qr = qrope_ref[...]
    qr_scr[...] = jnp.concatenate(
        [qr[:, ROPE_DIM * h:ROPE_DIM * (h + 1)] for h in range(MLA_HEADS)], axis=0).astype(BF16)
    kn_scr[...] = jnp.zeros(kn_scr.shape, F32)
    krn_scr[...] = jnp.zeros(krn_scr.shape, F32)
    kn_scr[:dec_seq, :] = ckvn_ref[...]
    krn_scr[:dec_seq, :] = krn_ref[...]
    _softmax_init(m_scr, l_scr, acc_scr)

    fetch_ahead_and_wait(0)
    score(0)

    def chunk_body(c, carry):
        fetch_ahead_and_wait(c)
        alpha, p = softmax()
        score(slot_of(c))
        weigh(slot_of(c - 1), alpha, p)
        return carry

    lax.fori_loop(1, n_chunks, chunk_body, 0)
    alpha, p = softmax()
    weigh(slot_of(n_chunks - 1), alpha, p)

    @pl.when(seq == n_seq - 1)
    def _():
        for c in range(n_chunks - FETCH_AHEAD, n_chunks):
            for cp in page_copies(seq, c, slot_of(c + FETCH_AHEAD)):
                cp.wait()

    ck = kn_scr[...].astype(BF16)
    s_new = _dot_nt(ql_scr[...], ck) + _dot_nt(qr_scr[...], krn_scr[...].astype(BF16))
    q_pos = _mod_pow2(lax.broadcasted_iota(jnp.int32, (rows, PAGE_SIZE), 0), dec_seq)
    k_pos = lax.broadcasted_iota(jnp.int32, (rows, PAGE_SIZE), 1)
    s_new = jnp.where(k_pos <= q_pos, s_new, NEG_INF)
    _, l_fin, alpha, p = _softmax_weights(s_new, m_scr[...], l_scr[...], MLA_EXP2_SCALE)
    acc_fin = _lane_repeat(alpha, KV_LORA) * acc_scr[...] + _dot(p, ck)
    o = acc_fin * _lane_repeat(1.0 / l_fin, KV_LORA)
    for h in range(MLA_HEADS):
        o_ref[h] = o[h * dec_seq:(h + 1) * dec_seq, :]


def _mla_decode(page_table, qlat, qrope, ckv_new, kr_new, cache_ckv, cache_krt, dec_seq, cpg):
    n_seq, n_pages = page_table.shape
    ntok = n_seq * dec_seq
    rows = MLA_HEADS * dec_seq
    tok = lambda width: pl.BlockSpec((dec_seq, width), lambda s, pt: (s, 0))
    head_tok = pl.BlockSpec((MLA_HEADS, dec_seq, KV_LORA), lambda s, pt: (0, s, 0))
    hbm = pl.BlockSpec(memory_space=pl.ANY)
    grid_spec = pltpu.PrefetchScalarGridSpec(
        num_scalar_prefetch=1,
        grid=(n_seq,),
        in_specs=[head_tok, tok(ROPE_LANES), tok(KV_LORA), tok(ROPE_DIM), hbm, hbm],
        out_specs=head_tok,
        scratch_shapes=[pltpu.VMEM((N_PAGE_SLOTS, cpg, PAGE_SIZE, KV_LORA), F32),
                        pltpu.VMEM((N_PAGE_SLOTS, cpg, ROPE_DIM, PAGE_SIZE), F32),
                        pltpu.SemaphoreType.DMA((N_PAGE_SLOTS, 2)),
                        pltpu.VMEM((rows, KV_LORA), BF16), pltpu.VMEM((rows, ROPE_DIM), BF16),
                        pltpu.VMEM((PAGE_SIZE, KV_LORA), F32), pltpu.VMEM((PAGE_SIZE, ROPE_DIM), F32),
                        pltpu.VMEM((rows, cpg * PAGE_SIZE), F32),
                        pltpu.VMEM((rows, LANES), F32), pltpu.VMEM((rows, LANES), F32),
                        pltpu.VMEM((rows, KV_LORA), F32)],
    )
    return pl.pallas_call(
        functools.partial(_mla_decode_kernel, cpg=cpg, n_chunks=n_pages // cpg, dec_seq=dec_seq),
        grid_spec=grid_spec,
        out_shape=jax.ShapeDtypeStruct((MLA_HEADS, ntok, KV_LORA), F32),
        compiler_params=_params(1),
        name="mla_decode",
    )(page_table, qlat, qrope, ckv_new, kr_new, cache_ckv, cache_krt)


def _uv_kernel(o_ref, wuv_ref, y_ref):
    y = _dot(o_ref[0].astype(BF16), wuv_ref[0])
    for h in range(1, MLA_HEADS):
        y = y + _dot(o_ref[h].astype(BF16), wuv_ref[h])
    y_ref[...] = y


def _uv(o_lat, wuv, tm):
    ntok = o_lat.shape[1]
    return pl.pallas_call(
        _uv_kernel,
        grid=(ntok // tm,),
        in_specs=[pl.BlockSpec((MLA_HEADS, tm, KV_LORA), lambda t: (0, t, 0)), _const_spec(wuv.shape)],
        out_specs=pl.BlockSpec((tm, MLA_WIDTH), lambda t: (t, 0)),
        out_shape=jax.ShapeDtypeStruct((ntok, MLA_WIDTH), F32),
        compiler_params=_params(1),
        name="uv",
    )(o_lat, wuv)


def _memkv_kernel(mem_ref, wk_ref, wv_ref, k_ref, v_ref):
    mb = mem_ref[...].astype(BF16)
    k_ref[...] = _dot(mb, wk_ref[...])
    v_ref[...] = _dot(mb, wv_ref[...])


def _memkv(mem2d, wk, wv, tm):
    n = mem2d.shape[0]
    out = pl.BlockSpec((tm, X_WIDTH), lambda t: (t, 0))
    return pl.pallas_call(
        _memkv_kernel,
        grid=(n // tm,),
        in_specs=[pl.BlockSpec((tm, D_MODEL), lambda t: (t, 0)), _const_spec(wk.shape),
                  _const_spec(wv.shape)],
        out_specs=[out, out],
        out_shape=[jax.ShapeDtypeStruct((n, X_WIDTH), F32)] * 2,
        compiler_params=_params(1),
        name="memkv",
    )(mem2d, wk, wv)


def _xattn_kernel(x_ref, wqx_ref, mk_ref, mv_ref, y_ref, *, groups, g, head_interleaved):
    qx = _dot(x_ref[...].astype(BF16), wqx_ref[...])
    units = [(s, h) for s in range(groups) for h in range(X_HEADS)]
    rows = lambda s: slice(s * g, (s + 1) * g)
    cols = lambda h: slice(h * X_HEAD_DIM, (h + 1) * X_HEAD_DIM)

    def head_of(ref, s, h):
        if head_interleaved:
            return ref[s, pl.ds(h, MEM_LEN, stride=X_HEADS), :].astype(BF16)
        return ref[s, :, cols(h)].astype(BF16)

    scores = [_dot_nt(qx[rows(s), cols(h)].astype(BF16), head_of(mk_ref, s, h)) for s, h in units]
    weights = []
    for sc in scores:
        sc = sc * X_SCALE
        p = jnp.exp(sc - jnp.max(sc, axis=-1, keepdims=True))
        weights.append((p / jnp.sum(p, axis=-1, keepdims=True)).astype(BF16))
    for (s, h), p in zip(units, weights):
        y_ref[rows(s), cols(h)] = _dot(p, head_of(mv_ref, s, h))


def _xattn(x2d, wqx, mem_k, mem_v, groups, g, tiles_per_mem):
    ntok = x2d.shape[0]
    tm = groups * g
    head_interleaved = mem_k.shape[-1] == X_HEAD_DIM
    mem = pl.BlockSpec((groups,) + mem_k.shape[1:], lambda t: (t // tiles_per_mem, 0, 0))
    return pl.pallas_call(
        functools.partial(_xattn_kernel, groups=groups, g=g, head_interleaved=head_interleaved),
        grid=(ntok // tm,),
        in_specs=[pl.BlockSpec((tm, D_MODEL), lambda t: (t, 0)), _const_spec(wqx.shape), mem, mem],
        out_specs=pl.BlockSpec((tm, X_WIDTH), lambda t: (t, 0)),
        out_shape=jax.ShapeDtypeStruct((ntok, X_WIDTH), F32),
        compiler_params=_params(1),
        name="xattn",
    )(x2d, wqx, mem_k, mem_v)


def _merge_kernel(x_ref, ymla_ref, yx_ref, prev_ref, wb_ref, wpm_ref, wpc_ref, wpx_ref, wo_ref,
                  convw_ref, lng_ref, lnb_ref, out_ref, pre_ref, carry_scr,
                  *, tm, seq_rows, tiles_per_seq, alpha):
    x = x_ref[...]
    xb = x.astype(BF16)

    def proj(k):
        lo = k * CONV_DIM if k < 6 else 6 * CONV_DIM + (k - 6) * D_MODEL
        hi = lo + (CONV_DIM if k < 6 else D_MODEL)
        return _dot(xb, wb_ref[:, lo:hi])

    def branch(y, z, gate, wp_ref):
        return jax.nn.sigmoid(gate) * _dot((y * jax.nn.silu(z)).astype(BF16), wp_ref[...])

    m = branch(ymla_ref[...], proj(0), proj(6), wpm_ref)

    pre = proj(2) * proj(3)
    row = lax.broadcasted_iota(jnp.int32, (tm, CONV_DIM), 0)
    if tiles_per_seq is not None:
        @pl.when(pl.program_id(0) % tiles_per_seq == 0)
        def _():
            carry_scr[...] = jnp.zeros(carry_scr.shape, F32)
        prev1 = jnp.broadcast_to(carry_scr[7:8, :], (tm, CONV_DIM))
        prev0 = jnp.broadcast_to(carry_scr[6:7, :], (tm, CONV_DIM))
        pos = row
    else:
        n_seq = tm // seq_rows
        prev = prev_ref[...]
        bc = lambda r: jnp.broadcast_to(prev[:, r:r + 1, :], (n_seq, seq_rows, CONV_DIM)).reshape(tm, CONV_DIM)
        prev0, prev1 = bc(0), bc(1)
        pos = _mod_pow2(row, seq_rows)
    back1 = jnp.where(pos == 0, prev1, pltpu.roll(pre, 1, 0))
    back2 = jnp.where(pos == 0, prev0, jnp.where(pos == 1, prev1, pltpu.roll(pre, 2, 0)))
    cw = convw_ref[...]
    conv = cw[0:1, :] * back2 + cw[1:2, :] * back1 + cw[2:3, :] * pre
    if tiles_per_seq is not None:
        pre_ref[...] = pre[tm - 8:, :]
        carry_scr[...] = pre[tm - 8:, :]
    else:
        pre_ref[...] = pre
    m = m + branch(proj(1) * conv, proj(4), proj(7), wpc_ref)

    m = m + branch(yx_ref[...], proj(5), proj(8), wpx_ref)

    r = alpha * x + _dot(m.astype(BF16), wo_ref[...])
    mu = jnp.mean(r, axis=-1, keepdims=True)
    d = r - mu
    var = jnp.mean(d * d, axis=-1, keepdims=True)
    out_ref[...] = d * lax.rsqrt(var + NORM_EPS) * lng_ref[...] + lnb_ref[...]


def _merge(x2d, ymla, yx, prev, w, tm, seq_rows, tiles_per_seq, alpha):
    ntok = x2d.shape[0]
    tok = lambda width: pl.BlockSpec((tm, width), lambda t: (t, 0))
    n_seq = max(tm // seq_rows, 1)
    prev_spec = pl.BlockSpec((n_seq, CONV_WIDTH - 1, CONV_DIM),
                             lambda t: (t if tiles_per_seq is None else 0, 0, 0))
    weights = [w["wb"], w["wpm"], w["wpc"], w["wpx"], w["wo"], w["convw"], w["lng"], w["lnb"]]
    if tiles_per_seq is None:
        pre_spec, pre_rows = tok(CONV_DIM), ntok
    else:
        pre_spec = pl.BlockSpec((8, CONV_DIM), lambda t: (t // tiles_per_seq, 0))
        pre_rows = 8 * (ntok // tm // tiles_per_seq)
    return pl.pallas_call(
        functools.partial(_merge_kernel, tm=tm, seq_rows=seq_rows, tiles_per_seq=tiles_per_seq,
                          alpha=alpha),
        grid=(ntok // tm,),
        in_specs=[tok(D_MODEL), tok(MLA_WIDTH), tok(X_WIDTH), prev_spec]
        + [_const_spec(a.shape) for a in weights],
        out_specs=[tok(D_MODEL), pre_spec],
        out_shape=[jax.ShapeDtypeStruct((ntok, D_MODEL), F32),
                   jax.ShapeDtypeStruct((pre_rows, CONV_DIM), F32)],
        scratch_shapes=[pltpu.VMEM((8, CONV_DIM), F32)],
        compiler_params=_params(1),
        name="merge",
    )(x2d, ymla, yx, prev, *weights)


def _swap_halves(w):
    half = ROPE_DIM // 2
    return jnp.concatenate([w[..., half:], w[..., :half]], axis=-1)


def _place_rope(w):
    z = jnp.zeros_like(w)
    return jnp.concatenate([w, z, w, z], axis=-1)


def _pack_weights(w_in, q_norm_g, kv_norm_g, w_uq, w_uk, w_uv, conv_w, w_mk, w_mv,
                  w_p_mla, w_p_conv, w_p_x, w_o, ln_g, ln_b):
    (w_cq, w_ckv, w_kr, w_zmla, w_cb, w_cc, w_ch, w_zc, w_qx, w_zx, w_g) = jnp.split(
        w_in, SPLIT_POINTS, axis=-1)
    tile_heads = lambda a: jnp.tile(a, (1, MLA_HEADS))
    uq = w_uq.reshape(Q_LORA, MLA_HEADS, NOPE_DIM + ROPE_DIM)
    uq_nope, uq_rope = uq[:, :, :NOPE_DIM], uq[:, :, NOPE_DIM:]
    uk_t = jnp.transpose(w_uk, (1, 2, 0))
    uv_t = jnp.transpose(w_uv, (1, 0, 2))
    wa = jnp.concatenate([w_cq, w_ckv, tile_heads(w_kr), tile_heads(_swap_halves(w_kr))], axis=1)
    wuq = jnp.concatenate([uq_nope.reshape(Q_LORA, -1), uq_rope.reshape(Q_LORA, -1),
                           _swap_halves(uq_rope).reshape(Q_LORA, -1)], axis=1)
    zeros = jnp.zeros((MLA_HEADS // 2, NOPE_DIM, KV_LORA), w_uk.dtype)
    wuk = jnp.concatenate([jnp.concatenate([uk_t[0::2], zeros], axis=2),
                           jnp.concatenate([zeros, uk_t[1::2]], axis=2)], axis=1)
    wuv = jnp.stack([jnp.pad(uv_t[h], ((0, 0), (h * V_DIM, MLA_WIDTH - (h + 1) * V_DIM)))
                     for h in range(MLA_HEADS)])
    wa_p = jnp.concatenate([w_cq, w_ckv, _place_rope(w_kr), _place_rope(_swap_halves(w_kr))], axis=1)
    pad_q = jnp.zeros((Q_LORA, MLA_HEADS, HEAD_LANES - NOPE_DIM - ROPE_DIM), w_uq.dtype)
    wuq_p = jnp.concatenate([
        jnp.concatenate([uq_nope, uq_rope, pad_q], axis=-1).reshape(Q_LORA, PACKED),
        jnp.concatenate([jnp.zeros_like(uq_nope), _swap_halves(uq_rope), pad_q],
                        axis=-1).reshape(Q_LORA, PACKED)], axis=1)
    wukt_p = jnp.concatenate([uk_t, jnp.zeros((MLA_HEADS, HEAD_LANES - NOPE_DIM, KV_LORA), w_uk.dtype)],
                             axis=1).reshape(PACKED, KV_LORA)
    wuv_p = jnp.concatenate(
        [jnp.pad(uv_t[h], ((0, 0), ((h % 2) * V_DIM, HEAD_LANES - (h % 2 + 1) * V_DIM)))
         for h in range(MLA_HEADS)], axis=1)
    wb = jnp.concatenate([w_zmla, w_cb, w_cc, w_ch, w_zc, w_zx, w_g], axis=1)
    bf = lambda a: a.astype(BF16)
    row = lambda a: a.reshape(1, -1).astype(F32)
    return dict(wa=bf(wa), gq=row(q_norm_g), gkv=row(kv_norm_g), wuq=bf(wuq), wuk=bf(wuk),
                wuv=bf(wuv), wa_p=bf(wa_p), wuq_p=bf(wuq_p), wukt_p=bf(wukt_p), wuv_p=bf(wuv_p),
                wqx=bf(w_qx), wmk=bf(w_mk), wmv=bf(w_mv), wb=bf(wb),
                wpm=bf(w_p_mla), wpc=bf(w_p_conv), wpx=bf(w_p_x), wo=bf(w_o),
                convw=conv_w.astype(F32), lng=row(ln_g), lnb=row(ln_b))


def _rope_cos_sin(pos):
    half = ROPE_DIM // 2
    inv_freq = ROPE_THETA ** (-jnp.arange(half, dtype=jnp.float32) * (2.0 / ROPE_DIM))
    ang = pos.astype(jnp.float32)[:, None] * inv_freq[None, :]
    cos, sin = jnp.cos(ang), jnp.sin(ang)
    return jnp.concatenate([cos, cos], axis=-1), jnp.concatenate([-sin, sin], axis=-1)


def _prompt_tables(pos):
    cos, sin = _rope_cos_sin(pos)
    n = pos.shape[0]
    ones = jnp.ones((n, NOPE_DIM), F32)
    z32 = jnp.zeros((n, ROPE_DIM), F32)
    z64 = jnp.zeros((n, NOPE_DIM), F32)
    cq = jnp.concatenate([ones, cos, z32], axis=-1) * MLA_EXP2_SCALE
    sq = jnp.concatenate([z64, sin, z32], axis=-1) * MLA_EXP2_SCALE
    return cq, sq, _place_rope(cos), _place_rope(sin)


def kernel(x_prompt, x_sample, mem_prompt, cache_ckv, cache_kr, cache_conv, cache_mem_k,
           cache_mem_v, page_table, w_in, q_norm_g, kv_norm_g, w_uq, w_uk, w_uv, conv_w,
           w_mk, w_mv, w_p_mla, w_p_conv, w_p_x, w_o, ln_g, ln_b):
    depth = w_in.shape[0]
    batch, seq, _ = x_prompt.shape
    dec_batch, dec_seq, _ = x_sample.shape
    past_len = page_table.shape[1] * PAGE_SIZE
    alpha = (2 * depth) ** 0.25
    tm_p, tq, tm_s, x_groups_s, cpg = 512, KT_TILE, 256, 8, 16

    tables_p = _prompt_tables(jnp.arange(seq, dtype=jnp.float32))
    cos_s, sin_s = _rope_cos_sin(past_len + jnp.arange(dec_seq, dtype=jnp.float32))
    cos_s = jnp.tile(cos_s, (tm_s // dec_seq, MLA_HEADS))
    sin_s = jnp.tile(sin_s, (tm_s // dec_seq, MLA_HEADS))

    hp = x_prompt.reshape(batch * seq, D_MODEL)
    hs = x_sample.reshape(dec_batch * dec_seq, D_MODEL)
    mem_shape = (MEM_LEN, X_HEADS, X_HEAD_DIM)
    outs = [[] for _ in range(8)]
    for l in range(depth):
        w = _pack_weights(w_in[l], q_norm_g[l], kv_norm_g[l], w_uq[l], w_uk[l], w_uv[l], conv_w[l],
                          w_mk[l], w_mv[l], w_p_mla[l], w_p_conv[l], w_p_x[l], w_o[l], ln_g[l], ln_b[l])
        qp, kt, v, ckv_p, kr_p = _qkv_prompt(hp, w, tables_p, tm_p, seq // tm_p)
        ymla = _mla_prompt(qp, kt, v, batch, seq, tq)
        memk, memv = _memkv(mem_prompt.reshape(batch * MEM_LEN, D_MODEL), w["wmk"], w["wmv"], MEM_LEN)
        yx = _xattn(hp, w["wqx"], memk.reshape(batch, MEM_LEN, X_WIDTH),
                    memv.reshape(batch, MEM_LEN, X_WIDTH), 1, tm_p, seq // tm_p)
        memk = memk.reshape(batch, *mem_shape)
        memv = memv.reshape(batch, *mem_shape)
        no_prev = jnp.zeros((1, CONV_WIDTH - 1, CONV_DIM), F32)
        hp, pre_p = _merge(hp, ymla, yx, no_prev, w, tm_p, seq, seq // tm_p, alpha)
        qlat, qrope, ckv_s, kr_s = _qkv_sample(hs, w, cos_s, sin_s, tm_s)
        o_lat = _mla_decode(page_table, qlat, qrope, ckv_s, kr_s, cache_ckv[l],
                            jnp.swapaxes(cache_kr[l], 1, 2), dec_seq, cpg)
        ymla = _uv(o_lat, w["wuv"], tm_s)
        interleaved = (dec_batch, MEM_LEN * X_HEADS, X_HEAD_DIM)
        yx = _xattn(hs, w["wqx"], cache_mem_k[l].reshape(interleaved),
                    cache_mem_v[l].reshape(interleaved), x_groups_s, dec_seq, 1)
        hs, pre_s = _merge(hs, ymla, yx, cache_conv[l], w, tm_s, dec_seq, None, alpha)

        keep = CONV_WIDTH - 1
        for lst, val in zip(outs, (
                ckv_p.reshape(batch, seq, KV_LORA), kr_p.reshape(batch, seq, ROPE_DIM),
                pre_p.reshape(batch, 8, CONV_DIM)[:, 8 - keep:],
                memk, memv,
                ckv_s.reshape(dec_batch, dec_seq, KV_LORA), kr_s.reshape(dec_batch, dec_seq, ROPE_DIM),
                pre_s.reshape(dec_batch, dec_seq, CONV_DIM)[:, dec_seq - keep:])):
            lst.append(val)
    return (hp.reshape(batch, seq, D_MODEL), hs.reshape(dec_batch, dec_seq, D_MODEL),
            *[jnp.stack(lst) for lst in outs])
```

```python
import functools
import math

import jax
import jax.numpy as jnp
import numpy as np
from jax import lax
from jax.experimental import pallas as pl
from jax.experimental.pallas import tpu as pltpu

D_MODEL = 1024
PAGE_SIZE = 128
MLA_HEADS = 8
NOPE_DIM = 64
ROPE_DIM = 32
V_DIM = 64
Q_LORA = 384
KV_LORA = 256
MLA_WIDTH = MLA_HEADS * V_DIM
MLA_SCALE = (NOPE_DIM + ROPE_DIM) ** -0.5
ROPE_THETA = 10000.0
CONV_DIM = D_MODEL // 2
CONV_WIDTH = 3
MEM_LEN = 256
X_HEADS = 4
X_HEAD_DIM = 128
X_WIDTH = X_HEADS * X_HEAD_DIM
X_SCALE = X_HEAD_DIM ** -0.5
N_BRANCH = 3
NORM_EPS = 1e-6
NEG_INF = -1e30
SPLIT_SIZES = (Q_LORA, KV_LORA, ROPE_DIM, MLA_WIDTH,
               CONV_DIM, CONV_DIM, CONV_DIM, CONV_DIM,
               X_WIDTH, X_WIDTH, N_BRANCH * D_MODEL)
SPLIT_POINTS = tuple(int(v) for v in np.cumsum(SPLIT_SIZES)[:-1])

LANES = 128
ROPE_LANES = MLA_HEADS * ROPE_DIM
HEAD_LANES = LANES
PACKED = MLA_HEADS * HEAD_LANES
KT_TILE = 256
FETCH_AHEAD = 2
N_PAGE_SLOTS = FETCH_AHEAD + 2
VMEM_LIMIT = 48 * 1024 * 1024
BF16 = jnp.bfloat16
F32 = jnp.float32
NT_DIMS = (((1,), (1,)), ((), ()))
MLA_EXP2_SCALE = MLA_SCALE * math.log2(math.e)


def _dot(a, b):
    return jnp.dot(a, b, preferred_element_type=F32)


def _dot_nt(a, b):
    return lax.dot_general(a, b, NT_DIMS, preferred_element_type=F32)


def _rms(x, g):
    return x * lax.rsqrt(jnp.mean(x * x, axis=-1, keepdims=True) + NORM_EPS) * g


def _const_spec(shape):
    return pl.BlockSpec(shape, lambda *_: (0,) * len(shape), pipeline_mode=pl.Buffered(1))


def _mod_pow2(x, n):
    assert n & (n - 1) == 0
    return x & (n - 1)


def _params(n_axes):
    return pltpu.CompilerParams(dimension_semantics=("arbitrary",) * n_axes,
                                vmem_limit_bytes=VMEM_LIMIT)


def _qkv_prompt_kernel(x_ref, wa_ref, gq_ref, gkv_ref, wuq_ref, wukt_ref, wuv_ref,
                       cq_ref, sq_ref, ck_ref, sk_ref,
                       qp_ref, kt_ref, v_ref, ckv_ref, kr_ref):
    xb = x_ref[...].astype(BF16)
    pa = _dot(xb, wa_ref[...])
    cqn = _rms(pa[:, :Q_LORA], gq_ref[...])
    qa = _dot(cqn.astype(BF16), wuq_ref[...])
    cq = cq_ref[...]
    sq = sq_ref[...]
    for h in range(MLA_HEADS):
        lo = h * HEAD_LANES
        qp_ref[h] = (qa[:, lo:lo + HEAD_LANES] * cq
                     + qa[:, PACKED + lo:PACKED + lo + HEAD_LANES] * sq).astype(BF16)
    ckvn = _rms(pa[:, Q_LORA:Q_LORA + KV_LORA], gkv_ref[...])
    ckv_ref[...] = ckvn
    vp = _dot(ckvn.astype(BF16), wuv_ref[...])
    for pair in range(MLA_HEADS // 2):
        v_ref[pair] = vp[:, pair * HEAD_LANES:(pair + 1) * HEAD_LANES].astype(BF16)
    k0 = Q_LORA + KV_LORA
    krot = pa[:, k0:k0 + HEAD_LANES] * ck_ref[...] + pa[:, k0 + HEAD_LANES:] * sk_ref[...]
    kr_ref[...] = krot[:, :ROPE_DIM]
    lane = lax.broadcasted_iota(jnp.int32, krot.shape, 1)
    k_rope = jnp.where(lane >= NOPE_DIM, krot, 0.0)
    for t in range(kt_ref.shape[0]):
        rows = slice(t * KT_TILE, (t + 1) * KT_TILE)
        k_nope_t = _dot(wukt_ref[...], ckvn[rows, :].T.astype(BF16))
        k_rope_t = k_rope[rows, :].T
        for h in range(MLA_HEADS):
            kt_ref[t, h] = (k_nope_t[h * HEAD_LANES:(h + 1) * HEAD_LANES, :] + k_rope_t).astype(BF16)


def _qkv_prompt(x2d, w, tables, tm, table_tiles):
    ntok = x2d.shape[0]
    tok = lambda width: pl.BlockSpec((tm, width), lambda t: (t, 0))
    tab = pl.BlockSpec((tm, HEAD_LANES), lambda t: (t % table_tiles, 0))
    head_tok = pl.BlockSpec((MLA_HEADS, tm, HEAD_LANES), lambda t: (0, t, 0))
    weights = [w["wa_p"], w["gq"], w["gkv"], w["wuq_p"], w["wukt_p"], w["wuv_p"]]
    return pl.pallas_call(
        _qkv_prompt_kernel,
        grid=(ntok // tm,),
        in_specs=[tok(D_MODEL)] + [_const_spec(a.shape) for a in weights] + [tab] * 4,
        out_specs=[head_tok,
                   pl.BlockSpec((tm // KT_TILE, MLA_HEADS, HEAD_LANES, KT_TILE), lambda t: (t, 0, 0, 0)),
                   pl.BlockSpec((MLA_HEADS // 2, tm, HEAD_LANES), lambda t: (0, t, 0)),
                   tok(KV_LORA), tok(ROPE_DIM)],
        out_shape=[jax.ShapeDtypeStruct((MLA_HEADS, ntok, HEAD_LANES), BF16),
                   jax.ShapeDtypeStruct((ntok // KT_TILE, MLA_HEADS, HEAD_LANES, KT_TILE), BF16),
                   jax.ShapeDtypeStruct((MLA_HEADS // 2, ntok, HEAD_LANES), BF16),
                   jax.ShapeDtypeStruct((ntok, KV_LORA), F32),
                   jax.ShapeDtypeStruct((ntok, ROPE_DIM), F32)],
        compiler_params=_params(1),
        name="qkv_prompt",
    )(x2d, *weights, *tables)


def _qkv_sample_kernel(x_ref, wa_ref, gq_ref, gkv_ref, wuq_ref, wuk_ref, cos_ref, sin_ref,
                       qlat_ref, qrope_ref, ckv_ref, kr_ref):
    cos = cos_ref[...]
    sin = sin_ref[...]
    xb = x_ref[...].astype(BF16)
    pa = _dot(xb, wa_ref[...])
    cqn = _rms(pa[:, :Q_LORA], gq_ref[...])
    qa = _dot(cqn.astype(BF16), wuq_ref[...])
    r0 = MLA_HEADS * NOPE_DIM
    qrope_ref[...] = qa[:, r0:r0 + ROPE_LANES] * cos + qa[:, r0 + ROPE_LANES:] * sin
    for p in range(MLA_HEADS // 2):
        ql = _dot(qa[:, 128 * p:128 * (p + 1)].astype(BF16), wuk_ref[p])
        qlat_ref[2 * p] = ql[:, :KV_LORA]
        qlat_ref[2 * p + 1] = ql[:, KV_LORA:]
    ckv_ref[...] = _rms(pa[:, Q_LORA:Q_LORA + KV_LORA], gkv_ref[...])
    k0 = Q_LORA + KV_LORA
    krt = pa[:, k0:k0 + ROPE_LANES] * cos + pa[:, k0 + ROPE_LANES:] * sin
    kr_ref[...] = krt[:, :ROPE_DIM]


def _qkv_sample(x2d, w, cos, sin, tm):
    ntok = x2d.shape[0]
    tok = lambda width: pl.BlockSpec((tm, width), lambda t: (t, 0))
    tab = pl.BlockSpec((tm, ROPE_LANES), lambda t: (0, 0))
    weights = [w["wa"], w["gq"], w["gkv"], w["wuq"], w["wuk"]]
    return pl.pallas_call(
        _qkv_sample_kernel,
        grid=(ntok // tm,),
        in_specs=[tok(D_MODEL)] + [_const_spec(a.shape) for a in weights] + [tab, tab],
        out_specs=[pl.BlockSpec((MLA_HEADS, tm, KV_LORA), lambda t: (0, t, 0)),
                   tok(ROPE_LANES), tok(KV_LORA), tok(ROPE_DIM)],
        out_shape=[jax.ShapeDtypeStruct((MLA_HEADS, ntok, KV_LORA), F32),
                   jax.ShapeDtypeStruct((ntok, ROPE_LANES), F32),
                   jax.ShapeDtypeStruct((ntok, KV_LORA), F32),
                   jax.ShapeDtypeStruct((ntok, ROPE_DIM), F32)],
        compiler_params=_params(1),
        name="qkv_sample",
    )(x2d, *weights, cos, sin)


def _lane_repeat(x, width):
    return jnp.concatenate([x] * (width // LANES), axis=-1)


def _softmax_weights(s, m_old, l_old, scale):
    mul = (lambda x: x) if scale == 1.0 else (lambda x: x * scale)
    m_new = jnp.maximum(m_old, jnp.max(s, axis=-1, keepdims=True))
    alpha = jnp.exp2(mul(m_old - m_new))
    p = jnp.exp2(mul(s - _lane_repeat(m_new, s.shape[-1])))
    l_new = alpha * l_old + jnp.sum(p, axis=-1, keepdims=True)
    return m_new, l_new, alpha, p.astype(BF16)


def _softmax_init(m_scr, l_scr, acc_scr):
    m_scr[...] = jnp.full(m_scr.shape, -jnp.inf, F32)
    l_scr[...] = jnp.zeros(l_scr.shape, F32)
    acc_scr[...] = jnp.zeros(acc_scr.shape, F32)


def _mla_prompt_kernel(qp_ref, kt_ref, v_ref, y_ref, m_scr, l_scr, acc_scr, *, tq, rq):
    assert tq == KT_TILE and tq % rq == 0
    i = pl.program_id(1)
    _softmax_init(m_scr, l_scr, acc_scr)
    chunks = [(h, r0) for h in range(MLA_HEADS) for r0 in range(0, tq, rq)]
    first_of_pair = lax.broadcasted_iota(jnp.int32, (rq, HEAD_LANES), 1) < V_DIM

    def score(j, n_tiles):
        return [jnp.concatenate([_dot(qp_ref[h, r0:r0 + rq, :], kt_ref[j + t, h])
                                 for t in range(n_tiles)], axis=1) for h, r0 in chunks]

    def softmax(scores, diagonal):
        weights = {}
        for (h, r0), s in zip(chunks, scores):
            rs = slice(r0, r0 + rq)
            if diagonal:
                q_pos = r0 + lax.broadcasted_iota(jnp.int32, (rq, tq), 0)
                k_pos = lax.broadcasted_iota(jnp.int32, (rq, tq), 1)
                s = jnp.where(k_pos <= q_pos, s, NEG_INF)
            m_scr[h, rs, :], l_scr[h, rs, :], alpha, p = _softmax_weights(
                s, m_scr[h, rs, :], l_scr[h, rs, :], 1.0)
            weights[h, r0] = (alpha, p)
        return weights

    def weigh(j, n_tiles, weights):
        keys = pl.ds(pl.multiple_of(j * tq, tq), n_tiles * tq)
        for pair in range(MLA_HEADS // 2):
            ha, hb = 2 * pair, 2 * pair + 1
            for r0 in range(0, tq, rq):
                rs = slice(r0, r0 + rq)
                (alpha_a, p_a), (alpha_b, p_b) = weights[ha, r0], weights[hb, r0]
                v_pair = v_ref[pair, keys, :]
                pv = jnp.where(first_of_pair, _dot(p_a, v_pair), _dot(p_b, v_pair))
                acc_scr[pair, rs, :] = (jnp.where(first_of_pair, alpha_a, alpha_b)
                                        * acc_scr[pair, rs, :] + pv)

    def step(j, n_tiles, diagonal):
        weigh(j, n_tiles, softmax(score(j, n_tiles), diagonal))

    def body(j2, carry):
        step(2 * j2, 2, False)
        return carry

    lax.fori_loop(0, lax.shift_right_logical(i, 1), body, 0)

    @pl.when((i & 1) == 1)
    def _():
        step(i - 1, 1, False)

    step(i, 1, True)
    first = lax.broadcasted_iota(jnp.int32, (tq, HEAD_LANES), 1) < V_DIM
    for pair in range(MLA_HEADS // 2):
        inv_l = jnp.where(first, 1.0 / l_scr[2 * pair], 1.0 / l_scr[2 * pair + 1])
        y_ref[:, pair * HEAD_LANES:(pair + 1) * HEAD_LANES] = acc_scr[pair] * inv_l


def _mla_prompt(qp, kt, v, batch, seq, tq):
    ntok = batch * seq
    nq = seq // tq
    return pl.pallas_call(
        functools.partial(_mla_prompt_kernel, tq=tq, rq=128),
        grid=(batch, nq),
        in_specs=[pl.BlockSpec((MLA_HEADS, tq, HEAD_LANES), lambda b, i: (0, b * nq + i, 0)),
                  pl.BlockSpec((seq // KT_TILE, MLA_HEADS, HEAD_LANES, KT_TILE),
                               lambda b, i: (b, 0, 0, 0)),
                  pl.BlockSpec((MLA_HEADS // 2, seq, HEAD_LANES), lambda b, i: (0, b, 0))],
        out_specs=pl.BlockSpec((tq, MLA_WIDTH), lambda b, i: (b * nq + i, 0)),
        out_shape=jax.ShapeDtypeStruct((ntok, MLA_WIDTH), F32),
        scratch_shapes=[pltpu.VMEM((MLA_HEADS, tq, LANES), F32),
                        pltpu.VMEM((MLA_HEADS, tq, LANES), F32),
                        pltpu.VMEM((MLA_HEADS // 2, tq, HEAD_LANES), F32)],
        compiler_params=_params(2),
        name="mla_prompt",
    )(qp, kt, v)


def _mla_decode_kernel(pt_ref, qlat_ref, qrope_ref, ckvn_ref, krn_ref, ckv_hbm, krt_hbm, o_ref,
                       ckv_buf, krt_buf, sem, ql_scr, qr_scr, kn_scr, krn_scr, s_scr,
                       m_scr, l_scr, acc_scr, *, cpg, n_chunks, dec_seq):
    assert n_chunks % N_PAGE_SLOTS == 0 and N_PAGE_SLOTS == FETCH_AHEAD + 2
    seq = pl.program_id(0)
    n_seq = pl.num_programs(0)
    rows = MLA_HEADS * dec_seq

    def slot_of(c):
        return _mod_pow2(c, N_PAGE_SLOTS)

    def page_copies(sq, chunk, slot):
        copies = []
        for k in range(cpg):
            page = pt_ref[sq, chunk * cpg + k]
            copies.append(pltpu.make_async_copy(ckv_hbm.at[page], ckv_buf.at[slot, k], sem.at[slot, 0]))
            copies.append(pltpu.make_async_copy(krt_hbm.at[page], krt_buf.at[slot, k], sem.at[slot, 1]))
        return copies

    def start_chunk(sq, chunk, slot):
        for n, cp in enumerate(page_copies(sq, chunk, slot)):
            cp.start(priority=(n // 2) % 2)

    def chunk_ahead(c):
        ahead = c + FETCH_AHEAD
        spills = ahead >= n_chunks
        more_seqs = seq + 1 < n_seq
        sq = jnp.where(spills & more_seqs, seq + 1, seq)
        ch = jnp.where(spills, jnp.where(more_seqs, ahead - n_chunks, c), ahead)
        return sq, ch

    def fetch_ahead_and_wait(c):
        start_chunk(*chunk_ahead(c), slot_of(c + FETCH_AHEAD))
        for cp in page_copies(seq, c, slot_of(c)):
            cp.wait()

    def score(slot):
        ql = ql_scr[...]
        qr = qr_scr[...]
        for k in range(cpg):
            s_scr[:, k * PAGE_SIZE:(k + 1) * PAGE_SIZE] = (
                _dot_nt(ql, ckv_buf[slot, k].astype(BF16)) + _dot(qr, krt_buf[slot, k].astype(BF16)))

    chunk_keys = cpg * PAGE_SIZE

    def softmax():
        m_scr[...], l_scr[...], alpha, p = _softmax_weights(
            s_scr[:, :chunk_keys], m_scr[...], l_scr[...], MLA_EXP2_SCALE)
        return alpha, p

    def weighted(slot, p):
        pv = _dot(p[:, :PAGE_SIZE], ckv_buf[slot, 0].astype(BF16))
        for k in range(1, cpg):
            pv = pv + _dot(p[:, PAGE_SIZE * k:PAGE_SIZE * (k + 1)], ckv_buf[slot, k].astype(BF16))
        return pv

    def weigh(slot, alpha, p):
        acc_scr[...] = _lane_repeat(alpha, KV_LORA) * acc_scr[...] + weighted(slot, p)

    @pl.when(seq == 0)
    def _():
        for c in range(FETCH_AHEAD):
            start_chunk(0, c, c)

    ql_scr[...] = jnp.concatenate([qlat_ref[h] for h in range(MLA_HEADS)], axis=0).astype(BF16)
    qr = qrope_ref[...]
    qr_scr[...] = jnp.concatenate(
        [qr[:, ROPE_DIM * h:ROPE_DIM * (h + 1)] for h in range(MLA_HEADS)], axis=0).astype(BF16)
    kn_scr[...] = jnp.zeros(kn_scr.shape, F32)
    krn_scr[...] = jnp.zeros(krn_scr.shape, F32)
    kn_scr[:dec_seq, :] = ckvn_ref[...]
    krn_scr[:dec_seq, :] = krn_ref[...]
    _softmax_init(m_scr, l_scr, acc_scr)
    ck_new = kn_scr[...].astype(BF16)
    s_new = _dot_nt(ql_scr[...], ck_new) + _dot_nt(qr_scr[...], krn_scr[...].astype(BF16))
    q_pos = _mod_pow2(lax.broadcasted_iota(jnp.int32, (rows, PAGE_SIZE), 0), dec_seq)
    k_pos = lax.broadcasted_iota(jnp.int32, (rows, PAGE_SIZE), 1)
    s_scr[:, chunk_keys:] = jnp.where(k_pos <= q_pos, s_new, NEG_INF)

    fetch_ahead_and_wait(0)
    score(0)

    def chunk_body(c, carry):
        fetch_ahead_and_wait(c)
        alpha, p = softmax()
        score(slot_of(c))
        weigh(slot_of(c - 1), alpha, p)
        return carry

    lax.fori_loop(1, n_chunks, chunk_body, 0)
    _, l_fin, alpha, p = _softmax_weights(s_scr[...], m_scr[...], l_scr[...], MLA_EXP2_SCALE)
    pv = weighted(slot_of(n_chunks - 1), p) + _dot(p[:, chunk_keys:], ck_new)
    o = (_lane_repeat(alpha, KV_LORA) * acc_scr[...] + pv) * _lane_repeat(1.0 / l_fin, KV_LORA)

    @pl.when(seq == n_seq - 1)
    def _():
        for c in range(n_chunks - FETCH_AHEAD, n_chunks):
            for cp in page_copies(seq, c, slot_of(c + FETCH_AHEAD)):
                cp.wait()

    for h in range(MLA_HEADS):
        o_ref[h] = o[h * dec_seq:(h + 1) * dec_seq, :]


def _mla_decode(page_table, qlat, qrope, ckv_new, kr_new, cache_ckv, cache_krt, dec_seq, cpg):
    n_seq, n_pages = page_table.shape
    ntok = n_seq * dec_seq
    rows = MLA_HEADS * dec_seq
    tok = lambda width: pl.BlockSpec((dec_seq, width), lambda s, pt: (s, 0))
    head_tok = pl.BlockSpec((MLA_HEADS, dec_seq, KV_LORA), lambda s, pt: (0, s, 0))
    hbm = pl.BlockSpec(memory_space=pl.ANY)
    grid_spec = pltpu.PrefetchScalarGridSpec(
        num_scalar_prefetch=1,
        grid=(n_seq,),
        in_specs=[head_tok, tok(ROPE_LANES), tok(KV_LORA), tok(ROPE_DIM), hbm, hbm],
        out_specs=head_tok,
        scratch_shapes=[pltpu.VMEM((N_PAGE_SLOTS, cpg, PAGE_SIZE, KV_LORA), F32),
                        pltpu.VMEM((N_PAGE_SLOTS, cpg, ROPE_DIM, PAGE_SIZE), F32),
                        pltpu.SemaphoreType.DMA((N_PAGE_SLOTS, 2)),
                        pltpu.VMEM((rows, KV_LORA), BF16), pltpu.VMEM((rows, ROPE_DIM), BF16),
                        pltpu.VMEM((PAGE_SIZE, KV_LORA), F32), pltpu.VMEM((PAGE_SIZE, ROPE_DIM), F32),
                        pltpu.VMEM((rows, (cpg + 1) * PAGE_SIZE), F32),
                        pltpu.VMEM((rows, LANES), F32), pltpu.VMEM((rows, LANES), F32),
                        pltpu.VMEM((rows, KV_LORA), F32)],
    )
    return pl.pallas_call(
        functools.partial(_mla_decode_kernel, cpg=cpg, n_chunks=n_pages // cpg, dec_seq=dec_seq),
        grid_spec=grid_spec,
        out_shape=jax.ShapeDtypeStruct((MLA_HEADS, ntok, KV_LORA), F32),
        compiler_params=_params(1),
        name="mla_decode",
    )(page_table, qlat, qrope, ckv_new, kr_new, cache_ckv, cache_krt)


def _uv_kernel(o_ref, wuv_ref, y_ref):
    y = _dot(o_ref[0].astype(BF16), wuv_ref[0])
    for h in range(1, MLA_HEADS):
        y = y + _dot(o_ref[h].astype(BF16), wuv_ref[h])
    y_ref[...] = y


def _uv(o_lat, wuv, tm):
    ntok = o_lat.shape[1]
    return pl.pallas_call(
        _uv_kernel,
        grid=(ntok // tm,),
        in_specs=[pl.BlockSpec((MLA_HEADS, tm, KV_LORA), lambda t: (0, t, 0)), _const_spec(wuv.shape)],
        out_specs=pl.BlockSpec((tm, MLA_WIDTH), lambda t: (t, 0)),
        out_shape=jax.ShapeDtypeStruct((ntok, MLA_WIDTH), F32),
        compiler_params=_params(1),
        name="uv",
    )(o_lat, wuv)


def _memkv_kernel(mem_ref, wk_ref, wv_ref, k_ref, v_ref):
    mb = mem_ref[...].astype(BF16)
    k_ref[...] = _dot(mb, wk_ref[...])
    v_ref[...] = _dot(mb, wv_ref[...])


def _memkv(mem2d, wk, wv, tm):
    n = mem2d.shape[0]
    out = pl.BlockSpec((tm, X_WIDTH), lambda t: (t, 0))
    return pl.pallas_call(
        _memkv_kernel,
        grid=(n // tm,),
        in_specs=[pl.BlockSpec((tm, D_MODEL), lambda t: (t, 0)), _const_spec(wk.shape),
                  _const_spec(wv.shape)],
        out_specs=[out, out],
        out_shape=[jax.ShapeDtypeStruct((n, X_WIDTH), F32)] * 2,
        compiler_params=_params(1),
        name="memkv",
    )(mem2d, wk, wv)


def _xattn_kernel(x_ref, wqx_ref, mk_ref, mv_ref, y_ref, *, groups, g, head_interleaved):
    qx = _dot(x_ref[...].astype(BF16), wqx_ref[...])
    units = [(s, h) for s in range(groups) for h in range(X_HEADS)]
    rows = lambda s: slice(s * g, (s + 1) * g)
    cols = lambda h: slice(h * X_HEAD_DIM, (h + 1) * X_HEAD_DIM)

    def head_of(ref, s, h):
        if head_interleaved:
            return ref[s, pl.ds(h, MEM_LEN, stride=X_HEADS), :].astype(BF16)
        return ref[s, :, cols(h)].astype(BF16)

    scores = [_dot_nt(qx[rows(s), cols(h)].astype(BF16), head_of(mk_ref, s, h)) for s, h in units]
    weights = []
    for sc in scores:
        sc = sc * X_SCALE
        p = jnp.exp(sc - jnp.max(sc, axis=-1, keepdims=True))
        weights.append((p / jnp.sum(p, axis=-1, keepdims=True)).astype(BF16))
    for (s, h), p in zip(units, weights):
        y_ref[rows(s), cols(h)] = _dot(p, head_of(mv_ref, s, h))


def _xattn(x2d, wqx, mem_k, mem_v, groups, g, tiles_per_mem):
    ntok = x2d.shape[0]
    tm = groups * g
    head_interleaved = mem_k.shape[-1] == X_HEAD_DIM
    mem = pl.BlockSpec((groups,) + mem_k.shape[1:], lambda t: (t // tiles_per_mem, 0, 0))
    return pl.pallas_call(
        functools.partial(_xattn_kernel, groups=groups, g=g, head_interleaved=head_interleaved),
        grid=(ntok // tm,),
        in_specs=[pl.BlockSpec((tm, D_MODEL), lambda t: (t, 0)), _const_spec(wqx.shape), mem, mem],
        out_specs=pl.BlockSpec((tm, X_WIDTH), lambda t: (t, 0)),
        out_shape=jax.ShapeDtypeStruct((ntok, X_WIDTH), F32),
        compiler_params=_params(1),
        name="xattn",
    )(x2d, wqx, mem_k, mem_v)


def _merge_kernel(x_ref, ymla_ref, yx_ref, prev_ref, wb_ref, wpm_ref, wpc_ref, wpx_ref, wo_ref,
                  convw_ref, lng_ref, lnb_ref, out_ref, pre_ref, carry_scr,
                  *, tm, seq_rows, tiles_per_seq, alpha):
    x = x_ref[...]
    xb = x.astype(BF16)

    def proj(k):
        lo = k * CONV_DIM if k < 6 else 6 * CONV_DIM + (k - 6) * D_MODEL
        hi = lo + (CONV_DIM if k < 6 else D_MODEL)
        return _dot(xb, wb_ref[:, lo:hi])

    def branch(y, z, gate, wp_ref):
        return jax.nn.sigmoid(gate) * _dot((y * jax.nn.silu(z)).astype(BF16), wp_ref[...])

    m = branch(ymla_ref[...], proj(0), proj(6), wpm_ref)

    pre = proj(2) * proj(3)
    row = lax.broadcasted_iota(jnp.int32, (tm, CONV_DIM), 0)
    if tiles_per_seq is not None:
        @pl.when(pl.program_id(0) % tiles_per_seq == 0)
        def _():
            carry_scr[...] = jnp.zeros(carry_scr.shape, F32)
        prev1 = jnp.broadcast_to(carry_scr[7:8, :], (tm, CONV_DIM))
        prev0 = jnp.broadcast_to(carry_scr[6:7, :], (tm, CONV_DIM))
        pos = row
    else:
        n_seq = tm // seq_rows
        prev = prev_ref[...]
        bc = lambda r: jnp.broadcast_to(prev[:, r:r + 1, :], (n_seq, seq_rows, CONV_DIM)).reshape(tm, CONV_DIM)
        prev0, prev1 = bc(0), bc(1)
        pos = _mod_pow2(row, seq_rows)
    back1 = jnp.where(pos == 0, prev1, pltpu.roll(pre, 1, 0))
    back2 = jnp.where(pos == 0, prev0, jnp.where(pos == 1, prev1, pltpu.roll(pre, 2, 0)))
    cw = convw_ref[...]
    conv = cw[0:1, :] * back2 + cw[1:2, :] * back1 + cw[2:3, :] * pre
    if tiles_per_seq is not None:
        pre_ref[...] = pre[tm - 8:, :]
        carry_scr[...] = pre[tm - 8:, :]
    else:
        pre_ref[...] = pre
    m = m + branch(proj(1) * conv, proj(4), proj(7), wpc_ref)

    m = m + branch(yx_ref[...], proj(5), proj(8), wpx_ref)

    r = alpha * x + _dot(m.astype(BF16), wo_ref[...])
    mu = jnp.mean(r, axis=-1, keepdims=True)
    d = r - mu
    var = jnp.mean(d * d, axis=-1, keepdims=True)
    out_ref[...] = d * lax.rsqrt(var + NORM_EPS) * lng_ref[...] + lnb_ref[...]


def _merge(x2d, ymla, yx, prev, w, tm, seq_rows, tiles_per_seq, alpha):
    ntok = x2d.shape[0]
    tok = lambda width: pl.BlockSpec((tm, width), lambda t: (t, 0))
    n_seq = max(tm // seq_rows, 1)
    prev_spec = pl.BlockSpec((n_seq, CONV_WIDTH - 1, CONV_DIM),
                             lambda t: (t if tiles_per_seq is None else 0, 0, 0))
    weights = [w["wb"], w["wpm"], w["wpc"], w["wpx"], w["wo"], w["convw"], w["lng"], w["lnb"]]
    if tiles_per_seq is None:
        pre_spec, pre_rows = tok(CONV_DIM), ntok
    else:
        pre_spec = pl.BlockSpec((8, CONV_DIM), lambda t: (t // tiles_per_seq, 0))
        pre_rows = 8 * (ntok // tm // tiles_per_seq)
    return pl.pallas_call(
        functools.partial(_merge_kernel, tm=tm, seq_rows=seq_rows, tiles_per_seq=tiles_per_seq,
                          alpha=alpha),
        grid=(ntok // tm,),
        in_specs=[tok(D_MODEL), tok(MLA_WIDTH), tok(X_WIDTH), prev_spec]
        + [_const_spec(a.shape) for a in weights],
        out_specs=[tok(D_MODEL), pre_spec],
        out_shape=[jax.ShapeDtypeStruct((ntok, D_MODEL), F32),
                   jax.ShapeDtypeStruct((pre_rows, CONV_DIM), F32)],
        scratch_shapes=[pltpu.VMEM((8, CONV_DIM), F32)],
        compiler_params=_params(1),
        name="merge",
    )(x2d, ymla, yx, prev, *weights)


def _swap_halves(w):
    half = ROPE_DIM // 2
    return jnp.concatenate([w[..., half:], w[..., :half]], axis=-1)


def _place_rope(w):
    z = jnp.zeros_like(w)
    return jnp.concatenate([w, z, w, z], axis=-1)


def _pack_weights(w_in, q_norm_g, kv_norm_g, w_uq, w_uk, w_uv, conv_w, w_mk, w_mv,
                  w_p_mla, w_p_conv, w_p_x, w_o, ln_g, ln_b):
    (w_cq, w_ckv, w_kr, w_zmla, w_cb, w_cc, w_ch, w_zc, w_qx, w_zx, w_g) = jnp.split(
        w_in, SPLIT_POINTS, axis=-1)
    tile_heads = lambda a: jnp.tile(a, (1, MLA_HEADS))
    uq = w_uq.reshape(Q_LORA, MLA_HEADS, NOPE_DIM + ROPE_DIM)
    uq_nope, uq_rope = uq[:, :, :NOPE_DIM], uq[:, :, NOPE_DIM:]
    uk_t = jnp.transpose(w_uk, (1, 2, 0))
    uv_t = jnp.transpose(w_uv, (1, 0, 2))
    wa = jnp.concatenate([w_cq, w_ckv, tile_heads(w_kr), tile_heads(_swap_halves(w_kr))], axis=1)
    wuq = jnp.concatenate([uq_nope.reshape(Q_LORA, -1), uq_rope.reshape(Q_LORA, -1),
                           _swap_halves(uq_rope).reshape(Q_LORA, -1)], axis=1)
    zeros = jnp.zeros((MLA_HEADS // 2, NOPE_DIM, KV_LORA), w_uk.dtype)
    wuk = jnp.concatenate([jnp.concatenate([uk_t[0::2], zeros], axis=2),
                           jnp.concatenate([zeros, uk_t[1::2]], axis=2)], axis=1)
    wuv = jnp.stack([jnp.pad(uv_t[h], ((0, 0), (h * V_DIM, MLA_WIDTH - (h + 1) * V_DIM)))
                     for h in range(MLA_HEADS)])
    wa_p = jnp.concatenate([w_cq, w_ckv, _place_rope(w_kr), _place_rope(_swap_halves(w_kr))], axis=1)
    pad_q = jnp.zeros((Q_LORA, MLA_HEADS, HEAD_LANES - NOPE_DIM - ROPE_DIM), w_uq.dtype)
    wuq_p = jnp.concatenate([
        jnp.concatenate([uq_nope, uq_rope, pad_q], axis=-1).reshape(Q_LORA, PACKED),
        jnp.concatenate([jnp.zeros_like(uq_nope), _swap_halves(uq_rope), pad_q],
                        axis=-1).reshape(Q_LORA, PACKED)], axis=1)
    wukt_p = jnp.concatenate([uk_t, jnp.zeros((MLA_HEADS, HEAD_LANES - NOPE_DIM, KV_LORA), w_uk.dtype)],
                             axis=1).reshape(PACKED, KV_LORA)
    wuv_p = w_uv.reshape(KV_LORA, MLA_WIDTH)
    wb = jnp.concatenate([w_zmla, w_cb, w_cc, w_ch, w_zc, w_zx, w_g], axis=1)
    bf = lambda a: a.astype(BF16)
    row = lambda a: a.reshape(1, -1).astype(F32)
    return dict(wa=bf(wa), gq=row(q_norm_g), gkv=row(kv_norm_g), wuq=bf(wuq), wuk=bf(wuk),
                wuv=bf(wuv), wa_p=bf(wa_p), wuq_p=bf(wuq_p), wukt_p=bf(wukt_p), wuv_p=bf(wuv_p),
                wqx=bf(w_qx), wmk=bf(w_mk), wmv=bf(w_mv), wb=bf(wb),
                wpm=bf(w_p_mla), wpc=bf(w_p_conv), wpx=bf(w_p_x), wo=bf(w_o),
                convw=conv_w.astype(F32), lng=row(ln_g), lnb=row(ln_b))


def _rope_cos_sin(pos):
    half = ROPE_DIM // 2
    inv_freq = ROPE_THETA ** (-jnp.arange(half, dtype=jnp.float32) * (2.0 / ROPE_DIM))
    ang = pos.astype(jnp.float32)[:, None] * inv_freq[None, :]
    cos, sin = jnp.cos(ang), jnp.sin(ang)
    return jnp.concatenate([cos, cos], axis=-1), jnp.concatenate([-sin, sin], axis=-1)


def _prompt_tables(pos):
    cos, sin = _rope_cos_sin(pos)
    n = pos.shape[0]
    ones = jnp.ones((n, NOPE_DIM), F32)
    z32 = jnp.zeros((n, ROPE_DIM), F32)
    z64 = jnp.zeros((n, NOPE_DIM), F32)
    cq = jnp.concatenate([ones, cos, z32], axis=-1) * MLA_EXP2_SCALE
    sq = jnp.concatenate([z64, sin, z32], axis=-1) * MLA_EXP2_SCALE
    return cq, sq, _place_rope(cos), _place_rope(sin)


def kernel(x_prompt, x_sample, mem_prompt, cache_ckv, cache_kr, cache_conv, cache_mem_k,
           cache_mem_v, page_table, w_in, q_norm_g, kv_norm_g, w_uq, w_uk, w_uv, conv_w,
           w_mk, w_mv, w_p_mla, w_p_conv, w_p_x, w_o, ln_g, ln_b):
    depth = w_in.shape[0]
    batch, seq, _ = x_prompt.shape
    dec_batch, dec_seq, _ = x_sample.shape
    past_len = page_table.shape[1] * PAGE_SIZE
    alpha = (2 * depth) ** 0.25
    tm_p, tq, tm_s, x_groups_s, cpg = 512, KT_TILE, 256, 8, 32

    tables_p = _prompt_tables(jnp.arange(seq, dtype=jnp.float32))
    cos_s, sin_s = _rope_cos_sin(past_len + jnp.arange(dec_seq, dtype=jnp.float32))
    cos_s = jnp.tile(cos_s, (tm_s // dec_seq, MLA_HEADS))
    sin_s = jnp.tile(sin_s, (tm_s // dec_seq, MLA_HEADS))

    hp = x_prompt.reshape(batch * seq, D_MODEL)
    hs = x_sample.reshape(dec_batch * dec_seq, D_MODEL)
    mem_shape = (MEM_LEN, X_HEADS, X_HEAD_DIM)
    outs = [[] for _ in range(8)]
    for l in range(depth):
        w = _pack_weights(w_in[l], q_norm_g[l], kv_norm_g[l], w_uq[l], w_uk[l], w_uv[l], conv_w[l],
                          w_mk[l], w_mv[l], w_p_mla[l], w_p_conv[l], w_p_x[l], w_o[l], ln_g[l], ln_b[l])
        qp, kt, v, ckv_p, kr_p = _qkv_prompt(hp, w, tables_p, tm_p, seq // tm_p)
        ymla = _mla_prompt(qp, kt, v, batch, seq, tq)
        memk, memv = _memkv(mem_prompt.reshape(batch * MEM_LEN, D_MODEL), w["wmk"], w["wmv"], MEM_LEN)
        yx = _xattn(hp, w["wqx"], memk.reshape(batch, MEM_LEN, X_WIDTH),
                    memv.reshape(batch, MEM_LEN, X_WIDTH), 1, tm_p, seq // tm_p)
        memk = memk.reshape(batch, *mem_shape)
        memv = memv.reshape(batch, *mem_shape)
        no_prev = jnp.zeros((1, CONV_WIDTH - 1, CONV_DIM), F32)
        hp, pre_p = _merge(hp, ymla, yx, no_prev, w, tm_p, seq, seq // tm_p, alpha)
        qlat, qrope, ckv_s, kr_s = _qkv_sample(hs, w, cos_s, sin_s, tm_s)
        o_lat = _mla_decode(page_table, qlat, qrope, ckv_s, kr_s, cache_ckv[l],
                            jnp.swapaxes(cache_kr[l], 1, 2), dec_seq, cpg)
        ymla = _uv(o_lat, w["wuv"], tm_s)
        interleaved = (dec_batch, MEM_LEN * X_HEADS, X_HEAD_DIM)
        yx = _xattn(hs, w["wqx"], cache_mem_k[l].reshape(interleaved),
                    cache_mem_v[l].reshape(interleaved), x_groups_s, dec_seq, 1)
        hs, pre_s = _merge(hs, ymla, yx, cache_conv[l], w, tm_s, dec_seq, None, alpha)

        keep = CONV_WIDTH - 1
        for lst, val in zip(outs, (
                ckv_p.reshape(batch, seq, KV_LORA), kr_p.reshape(batch, seq, ROPE_DIM),
                pre_p.reshape(batch, 8, CONV_DIM)[:, 8 - keep:],
                memk, memv,
                ckv_s.reshape(dec_batch, dec_seq, KV_LORA), kr_s.reshape(dec_batch, dec_seq, ROPE_DIM),
                pre_s.reshape(dec_batch, dec_seq, CONV_DIM)[:, dec_seq - keep:])):
            lst.append(val)
    return (hp.reshape(batch, seq, D_MODEL), hs.reshape(dec_batch, dec_seq, D_MODEL),
            *[jnp.stack(lst) for lst in outs])
```

```python
import functools
import math

import jax
import jax.numpy as jnp
import numpy as np
from jax import lax
from jax.experimental import pallas as pl
from jax.experimental.pallas import tpu as pltpu

D_MODEL = 1024
PAGE_SIZE = 128
MLA_HEADS = 8
NOPE_DIM = 64
ROPE_DIM = 32
V_DIM = 64
Q_LORA = 384
KV_LORA = 256
MLA_WIDTH = MLA_HEADS * V_DIM
MLA_SCALE = (NOPE_DIM + ROPE_DIM) ** -0.5
ROPE_THETA = 10000.0
CONV_DIM = D_MODEL // 2
CONV_WIDTH = 3
MEM_LEN = 256
X_HEADS = 4
X_HEAD_DIM = 128
X_WIDTH = X_HEADS * X_HEAD_DIM
X_SCALE = X_HEAD_DIM ** -0.5
N_BRANCH = 3
NORM_EPS = 1e-6
NEG_INF = -1e30
SPLIT_SIZES = (Q_LORA, KV_LORA, ROPE_DIM, MLA_WIDTH,
               CONV_DIM, CONV_DIM, CONV_DIM, CONV_DIM,
               X_WIDTH, X_WIDTH, N_BRANCH * D_MODEL)
SPLIT_POINTS = tuple(int(v) for v in np.cumsum(SPLIT_SIZES)[:-1])

LANES = 128
ROPE_LANES = MLA_HEADS * ROPE_DIM
HEAD_LANES = LANES
PACKED = MLA_HEADS * HEAD_LANES
KT_TILE = 256
FETCH_AHEAD = 2
N_PAGE_SLOTS = FETCH_AHEAD + 2
VMEM_LIMIT = 48 * 1024 * 1024
BF16 = jnp.bfloat16
F32 = jnp.float32
NT_DIMS = (((1,), (1,)), ((), ()))
MLA_EXP2_SCALE = MLA_SCALE * math.log2(math.e)


def _dot(a, b):
    return jnp.dot(a, b, preferred_element_type=F32)


def _dot_nt(a, b):
    return lax.dot_general(a, b, NT_DIMS, preferred_element_type=F32)


def _rms(x, g):
    return x * lax.rsqrt(jnp.mean(x * x, axis=-1, keepdims=True) + NORM_EPS) * g


def _const_spec(shape):
    return pl.BlockSpec(shape, lambda *_: (0,) * len(shape), pipeline_mode=pl.Buffered(1))


def _mod_pow2(x, n):
    assert n & (n - 1) == 0
    return x & (n - 1)


def _params(n_axes):
    return pltpu.CompilerParams(dimension_semantics=("arbitrary",) * n_axes,
                                vmem_limit_bytes=VMEM_LIMIT)


def _qkv_prompt_kernel(x_ref, wa_ref, gq_ref, gkv_ref, wuq_ref, wukt_ref, wuv_ref,
                       cq_ref, sq_ref, ck_ref, sk_ref,
                       qp_ref, kt_ref, v_ref, ckv_ref, kr_ref):
    xb = x_ref[...].astype(BF16)
    pa = _dot(xb, wa_ref[...])
    cqn = _rms(pa[:, :Q_LORA], gq_ref[...])
    qa = _dot(cqn.astype(BF16), wuq_ref[...])
    cq = cq_ref[...]
    sq = sq_ref[...]
    for h in range(MLA_HEADS):
        lo = h * HEAD_LANES
        qp_ref[h] = (qa[:, lo:lo + HEAD_LANES] * cq
                     + qa[:, PACKED + lo:PACKED + lo + HEAD_LANES] * sq).astype(BF16)
    ckvn = _rms(pa[:, Q_LORA:Q_LORA + KV_LORA], gkv_ref[...])
    ckv_ref[...] = ckvn
    vp = _dot(ckvn.astype(BF16), wuv_ref[...])
    for pair in range(MLA_HEADS // 2):
        v_ref[pair] = vp[:, pair * HEAD_LANES:(pair + 1) * HEAD_LANES].astype(BF16)
    k0 = Q_LORA + KV_LORA
    krot = pa[:, k0:k0 + HEAD_LANES] * ck_ref[...] + pa[:, k0 + HEAD_LANES:] * sk_ref[...]
    kr_ref[...] = krot[:, :ROPE_DIM]
    lane = lax.broadcasted_iota(jnp.int32, krot.shape, 1)
    k_rope = jnp.where(lane >= NOPE_DIM, krot, 0.0)
    for t in range(kt_ref.shape[0]):
        rows = slice(t * KT_TILE, (t + 1) * KT_TILE)
        k_nope_t = _dot(wukt_ref[...], ckvn[rows, :].T.astype(BF16))
        k_rope_t = k_rope[rows, :].T
        for h in range(MLA_HEADS):
            kt_ref[t, h] = (k_nope_t[h * HEAD_LANES:(h + 1) * HEAD_LANES, :] + k_rope_t).astype(BF16)


def _qkv_prompt(x2d, w, tables, tm, table_tiles):
    ntok = x2d.shape[0]
    tok = lambda width: pl.BlockSpec((tm, width), lambda t: (t, 0))
    tab = pl.BlockSpec((tm, HEAD_LANES), lambda t: (t % table_tiles, 0))
    head_tok = pl.BlockSpec((MLA_HEADS, tm, HEAD_LANES), lambda t: (0, t, 0))
    weights = [w["wa_p"], w["gq"], w["gkv"], w["wuq_p"], w["wukt_p"], w["wuv_p"]]
    return pl.pallas_call(
        _qkv_prompt_kernel,
        grid=(ntok // tm,),
        in_specs=[tok(D_MODEL)] + [_const_spec(a.shape) for a in weights] + [tab] * 4,
        out_specs=[head_tok,
                   pl.BlockSpec((tm // KT_TILE, MLA_HEADS, HEAD_LANES, KT_TILE), lambda t: (t, 0, 0, 0)),
                   pl.BlockSpec((MLA_HEADS // 2, tm, HEAD_LANES), lambda t: (0, t, 0)),
                   tok(KV_LORA), tok(ROPE_DIM)],
        out_shape=[jax.ShapeDtypeStruct((MLA_HEADS, ntok, HEAD_LANES), BF16),
                   jax.ShapeDtypeStruct((ntok // KT_TILE, MLA_HEADS, HEAD_LANES, KT_TILE), BF16),
                   jax.ShapeDtypeStruct((MLA_HEADS // 2, ntok, HEAD_LANES), BF16),
                   jax.ShapeDtypeStruct((ntok, KV_LORA), F32),
                   jax.ShapeDtypeStruct((ntok, ROPE_DIM), F32)],
        compiler_params=_params(1),
        name="qkv_prompt",
    )(x2d, *weights, *tables)


def _qkv_sample_kernel(x_ref, wa_ref, gq_ref, gkv_ref, wuq_ref, wuk_ref, cos_ref, sin_ref,
                       qlat_ref, qrope_ref, ckv_ref, kr_ref):
    cos = cos_ref[...]
    sin = sin_ref[...]
    xb = x_ref[...].astype(BF16)
    pa = _dot(xb, wa_ref[...])
    cqn = _rms(pa[:, :Q_LORA], gq_ref[...])
    qa = _dot(cqn.astype(BF16), wuq_ref[...])
    r0 = MLA_HEADS * NOPE_DIM
    qrope_ref[...] = qa[:, r0:r0 + ROPE_LANES] * cos + qa[:, r0 + ROPE_LANES:] * sin
    for p in range(MLA_HEADS // 2):
        ql = _dot(qa[:, 128 * p:128 * (p + 1)].astype(BF16), wuk_ref[p])
        qlat_ref[2 * p] = ql[:, :KV_LORA]
        qlat_ref[2 * p + 1] = ql[:, KV_LORA:]
    ckv_ref[...] = _rms(pa[:, Q_LORA:Q_LORA + KV_LORA], gkv_ref[...])
    k0 = Q_LORA + KV_LORA
    krt = pa[:, k0:k0 + ROPE_LANES] * cos + pa[:, k0 + ROPE_LANES:] * sin
    kr_ref[...] = krt[:, :ROPE_DIM]


def _qkv_sample(x2d, w, cos, sin, tm):
    ntok = x2d.shape[0]
    tok = lambda width: pl.BlockSpec((tm, width), lambda t: (t, 0))
    tab = pl.BlockSpec((tm, ROPE_LANES), lambda t: (0, 0))
    weights = [w["wa"], w["gq"], w["gkv"], w["wuq"], w["wuk"]]
    return pl.pallas_call(
        _qkv_sample_kernel,
        grid=(ntok // tm,),
        in_specs=[tok(D_MODEL)] + [_const_spec(a.shape) for a in weights] + [tab, tab],
        out_specs=[pl.BlockSpec((MLA_HEADS, tm, KV_LORA), lambda t: (0, t, 0)),
                   tok(ROPE_LANES), tok(KV_LORA), tok(ROPE_DIM)],
        out_shape=[jax.ShapeDtypeStruct((MLA_HEADS, ntok, KV_LORA), F32),
                   jax.ShapeDtypeStruct((ntok, ROPE_LANES), F32),
                   jax.ShapeDtypeStruct((ntok, KV_LORA), F32),
                   jax.ShapeDtypeStruct((ntok, ROPE_DIM), F32)],
        compiler_params=_params(1),
        name="qkv_sample",
    )(x2d, *weights, cos, sin)


def _lane_repeat(x, width):
    return jnp.concatenate([x] * (width // LANES), axis=-1)


def _softmax_weights(s, m_old, l_old, scale):
    mul = (lambda x: x) if scale == 1.0 else (lambda x: x * scale)
    m_new = jnp.maximum(m_old, jnp.max(s, axis=-1, keepdims=True))
    alpha = jnp.exp2(mul(m_old - m_new))
    p = jnp.exp2(mul(s - _lane_repeat(m_new, s.shape[-1])))
    l_new = alpha * l_old + jnp.sum(p, axis=-1, keepdims=True)
    return m_new, l_new, alpha, p.astype(BF16)


def _softmax_init(m_scr, l_scr, acc_scr):
    m_scr[...] = jnp.full(m_scr.shape, -jnp.inf, F32)
    l_scr[...] = jnp.zeros(l_scr.shape, F32)
    acc_scr[...] = jnp.zeros(acc_scr.shape, F32)


def _mla_prompt_kernel(qp_ref, kt_ref, v_ref, y_ref, m_scr, l_scr, acc_scr, *, tq, rq):
    assert tq == KT_TILE and tq % rq == 0
    i = pl.program_id(1)
    _softmax_init(m_scr, l_scr, acc_scr)
    chunks = [(h, r0) for h in range(MLA_HEADS) for r0 in range(0, tq, rq)]
    first_of_pair = lax.broadcasted_iota(jnp.int32, (rq, HEAD_LANES), 1) < V_DIM

    def score(j, n_tiles):
        return [jnp.concatenate([_dot(qp_ref[h, r0:r0 + rq, :], kt_ref[j + t, h])
                                 for t in range(n_tiles)], axis=1) for h, r0 in chunks]

    def softmax(scores, diagonal):
        weights = {}
        for (h, r0), s in zip(chunks, scores):
            rs = slice(r0, r0 + rq)
            if diagonal:
                q_pos = r0 + lax.broadcasted_iota(jnp.int32, (rq, tq), 0)
                k_pos = lax.broadcasted_iota(jnp.int32, (rq, tq), 1)
                s = jnp.where(k_pos <= q_pos, s, NEG_INF)
            m_scr[h, rs, :], l_scr[h, rs, :], alpha, p = _softmax_weights(
                s, m_scr[h, rs, :], l_scr[h, rs, :], 1.0)
            weights[h, r0] = (alpha, p)
        return weights

    def weigh(j, n_tiles, weights):
        keys = pl.ds(pl.multiple_of(j * tq, tq), n_tiles * tq)
        for pair in range(MLA_HEADS // 2):
            ha, hb = 2 * pair, 2 * pair + 1
            for r0 in range(0, tq, rq):
                rs = slice(r0, r0 + rq)
                (alpha_a, p_a), (alpha_b, p_b) = weights[ha, r0], weights[hb, r0]
                v_pair = v_ref[pair, keys, :]
                pv = jnp.where(first_of_pair, _dot(p_a, v_pair), _dot(p_b, v_pair))
                acc_scr[pair, rs, :] = (jnp.where(first_of_pair, alpha_a, alpha_b)
                                        * acc_scr[pair, rs, :] + pv)

    def step(j, n_tiles, diagonal):
        weigh(j, n_tiles, softmax(score(j, n_tiles), diagonal))

    def body(j2, carry):
        step(2 * j2, 2, False)
        return carry

    lax.fori_loop(0, lax.shift_right_logical(i, 1), body, 0)

    @pl.when((i & 1) == 1)
    def _():
        step(i - 1, 1, False)

    step(i, 1, True)
    first = lax.broadcasted_iota(jnp.int32, (tq, HEAD_LANES), 1) < V_DIM
    for pair in range(MLA_HEADS // 2):
        inv_l = jnp.where(first, 1.0 / l_scr[2 * pair], 1.0 / l_scr[2 * pair + 1])
        y_ref[:, pair * HEAD_LANES:(pair + 1) * HEAD_LANES] = acc_scr[pair] * inv_l


def _mla_prompt(qp, kt, v, batch, seq, tq):
    ntok = batch * seq
    nq = seq // tq
    return pl.pallas_call(
        functools.partial(_mla_prompt_kernel, tq=tq, rq=256),
        grid=(batch, nq),
        in_specs=[pl.BlockSpec((MLA_HEADS, tq, HEAD_LANES), lambda b, i: (0, b * nq + i, 0)),
                  pl.BlockSpec((seq // KT_TILE, MLA_HEADS, HEAD_LANES, KT_TILE),
                               lambda b, i: (b, 0, 0, 0)),
                  pl.BlockSpec((MLA_HEADS // 2, seq, HEAD_LANES), lambda b, i: (0, b, 0))],
        out_specs=pl.BlockSpec((tq, MLA_WIDTH), lambda b, i: (b * nq + i, 0)),
        out_shape=jax.ShapeDtypeStruct((ntok, MLA_WIDTH), F32),
        scratch_shapes=[pltpu.VMEM((MLA_HEADS, tq, LANES), F32),
                        pltpu.VMEM((MLA_HEADS, tq, LANES), F32),
                        pltpu.VMEM((MLA_HEADS // 2, tq, HEAD_LANES), F32)],
        compiler_params=_params(2),
        name="mla_prompt",
    )(qp, kt, v)


def _mla_decode_kernel(pt_ref, qlat_ref, qrope_ref, ckvn_ref, krn_ref, ckv_hbm, krt_hbm, o_ref,
                       ckv_buf, krt_buf, sem, ql_scr, qr_scr, kn_scr, krn_scr, s_scr,
                       m_scr, l_scr, acc_scr, *, cpg, n_chunks, dec_seq):
    assert n_chunks % N_PAGE_SLOTS == 0 and N_PAGE_SLOTS == FETCH_AHEAD + 2
    seq = pl.program_id(0)
    n_seq = pl.num_programs(0)
    rows = MLA_HEADS * dec_seq

    def slot_of(c):
        return _mod_pow2(c, N_PAGE_SLOTS)

    def page_copies(sq, chunk, slot):
        copies = []
        for k in range(cpg):
            page = pt_ref[sq, chunk * cpg + k]
            copies.append(pltpu.make_async_copy(ckv_hbm.at[page], ckv_buf.at[slot, k], sem.at[slot, 0]))
            lanes = pl.ds(k * PAGE_SIZE, PAGE_SIZE)
            copies.append(pltpu.make_async_copy(krt_hbm.at[page], krt_buf.at[slot, :, lanes], sem.at[slot, 1]))
        return copies

    def start_chunk(sq, chunk, slot):
        for n, cp in enumerate(page_copies(sq, chunk, slot)):
            cp.start(priority=(n // 2) % 2)

    def chunk_ahead(c):
        ahead = c + FETCH_AHEAD
        spills = ahead >= n_chunks
        more_seqs = seq + 1 < n_seq
        sq = jnp.where(spills & more_seqs, seq + 1, seq)
        ch = jnp.where(spills, jnp.where(more_seqs, ahead - n_chunks, c), ahead)
        return sq, ch

    def fetch_ahead_and_wait(c):
        start_chunk(*chunk_ahead(c), slot_of(c + FETCH_AHEAD))
        for cp in page_copies(seq, c, slot_of(c)):
            cp.wait()

    def score(slot):
        ql = ql_scr[...]
        rope = _dot(qr_scr[...], krt_buf[slot].astype(BF16))
        for k in range(cpg):
            keys = slice(k * PAGE_SIZE, (k + 1) * PAGE_SIZE)
            s_scr[:, keys] = _dot_nt(ql, ckv_buf[slot, k].astype(BF16)) + rope[:, keys]

    chunk_keys = cpg * PAGE_SIZE

    def softmax():
        m_scr[...], l_scr[...], alpha, p = _softmax_weights(
            s_scr[:, :chunk_keys], m_scr[...], l_scr[...], MLA_EXP2_SCALE)
        return alpha, p

    def weighted(slot, p):
        pv = _dot(p[:, :PAGE_SIZE], ckv_buf[slot, 0].astype(BF16))
        for k in range(1, cpg):
            pv = pv + _dot(p[:, PAGE_SIZE * k:PAGE_SIZE * (k + 1)], ckv_buf[slot, k].astype(BF16))
        return pv

    def weigh(slot, alpha, p):
        acc_scr[...] = _lane_repeat(alpha, KV_LORA) * acc_scr[...] + weighted(slot, p)

    @pl.when(seq == 0)
    def _():
        for c in range(FETCH_AHEAD):
            start_chunk(0, c, c)

    ql_scr[...] = jnp.concatenate([qlat_ref[h] for h in range(MLA_HEADS)], axis=0).astype(BF16)
    qr = qrope_ref[...]
    qr_scr[...] = jnp.concatenate(
        [qr[:, ROPE_DIM * h:ROPE_DIM * (h + 1)] for h in range(MLA_HEADS)], axis=0).astype(BF16)
    kn_scr[...] = jnp.zeros(kn_scr.shape, F32)
    krn_scr[...] = jnp.zeros(krn_scr.shape, F32)
    kn_scr[:dec_seq, :] = ckvn_ref[...]
    krn_scr[:dec_seq, :] = krn_ref[...]
    _softmax_init(m_scr, l_scr, acc_scr)
    ck_new = kn_scr[...].astype(BF16)
    s_new = _dot_nt(ql_scr[...], ck_new) + _dot_nt(qr_scr[...], krn_scr[...].astype(BF16))
    q_pos = _mod_pow2(lax.broadcasted_iota(jnp.int32, (rows, PAGE_SIZE), 0), dec_seq)
    k_pos = lax.broadcasted_iota(jnp.int32, (rows, PAGE_SIZE), 1)
    s_scr[:, chunk_keys:] = jnp.where(k_pos <= q_pos, s_new, NEG_INF)

    fetch_ahead_and_wait(0)
    score(0)

    def chunk_body(c, carry):
        fetch_ahead_and_wait(c)
        alpha, p = softmax()
        score(slot_of(c))
        weigh(slot_of(c - 1), alpha, p)
        return carry

    lax.fori_loop(1, n_chunks, chunk_body, 0)
    _, l_fin, alpha, p = _softmax_weights(s_scr[...], m_scr[...], l_scr[...], MLA_EXP2_SCALE)
    pv = weighted(slot_of(n_chunks - 1), p) + _dot(p[:, chunk_keys:], ck_new)
    o = (_lane_repeat(alpha, KV_LORA) * acc_scr[...] + pv) * _lane_repeat(1.0 / l_fin, KV_LORA)

    @pl.when(seq == n_seq - 1)
    def _():
        for c in range(n_chunks - FETCH_AHEAD, n_chunks):
            for cp in page_copies(seq, c, slot_of(c + FETCH_AHEAD)):
                cp.wait()

    for h in range(MLA_HEADS):
        o_ref[h] = o[h * dec_seq:(h + 1) * dec_seq, :]


def _mla_decode(page_table, qlat, qrope, ckv_new, kr_new, cache_ckv, cache_krt, dec_seq, cpg):
    n_seq, n_pages = page_table.shape
    ntok = n_seq * dec_seq
    rows = MLA_HEADS * dec_seq
    tok = lambda width: pl.BlockSpec((dec_seq, width), lambda s, pt: (s, 0))
    head_tok = pl.BlockSpec((MLA_HEADS, dec_seq, KV_LORA), lambda s, pt: (0, s, 0))
    hbm = pl.BlockSpec(memory_space=pl.ANY)
    grid_spec = pltpu.PrefetchScalarGridSpec(
        num_scalar_prefetch=1,
        grid=(n_seq,),
        in_specs=[head_tok, tok(ROPE_LANES), tok(KV_LORA), tok(ROPE_DIM), hbm, hbm],
        out_specs=head_tok,
        scratch_shapes=[pltpu.VMEM((N_PAGE_SLOTS, cpg, PAGE_SIZE, KV_LORA), F32),
                        pltpu.VMEM((N_PAGE_SLOTS, ROPE_DIM, cpg * PAGE_SIZE), F32),
                        pltpu.SemaphoreType.DMA((N_PAGE_SLOTS, 2)),
                        pltpu.VMEM((rows, KV_LORA), BF16), pltpu.VMEM((rows, ROPE_DIM), BF16),
                        pltpu.VMEM((PAGE_SIZE, KV_LORA), F32), pltpu.VMEM((PAGE_SIZE, ROPE_DIM), F32),
                        pltpu.VMEM((rows, (cpg + 1) * PAGE_SIZE), F32),
                        pltpu.VMEM((rows, LANES), F32), pltpu.VMEM((rows, LANES), F32),
                        pltpu.VMEM((rows, KV_LORA), F32)],
    )
    return pl.pallas_call(
        functools.partial(_mla_decode_kernel, cpg=cpg, n_chunks=n_pages // cpg, dec_seq=dec_seq),
        grid_spec=grid_spec,
        out_shape=jax.ShapeDtypeStruct((MLA_HEADS, ntok, KV_LORA), F32),
        compiler_params=_params(1),
        name="mla_decode",
    )(page_table, qlat, qrope, ckv_new, kr_new, cache_ckv, cache_krt)


def _uv_kernel(o_ref, wuv_ref, y_ref):
    y = _dot(o_ref[0].astype(BF16), wuv_ref[0])
    for h in range(1, MLA_HEADS):
        y = y + _dot(o_ref[h].astype(BF16), wuv_ref[h])
    y_ref[...] = y


def _uv(o_lat, wuv, tm):
    ntok = o_lat.shape[1]
    return pl.pallas_call(
        _uv_kernel,
        grid=(ntok // tm,),
        in_specs=[pl.BlockSpec((MLA_HEADS, tm, KV_LORA), lambda t: (0, t, 0)), _const_spec(wuv.shape)],
        out_specs=pl.BlockSpec((tm, MLA_WIDTH), lambda t: (t, 0)),
        out_shape=jax.ShapeDtypeStruct((ntok, MLA_WIDTH), F32),
        compiler_params=_params(1),
        name="uv",
    )(o_lat, wuv)


def _memkv_kernel(mem_ref, wk_ref, wv_ref, k_ref, v_ref):
    mb = mem_ref[...].astype(BF16)
    k_ref[...] = _dot(mb, wk_ref[...])
    v_ref[...] = _dot(mb, wv_ref[...])


def _memkv(mem2d, wk, wv, tm):
    n = mem2d.shape[0]
    out = pl.BlockSpec((tm, X_WIDTH), lambda t: (t, 0))
    return pl.pallas_call(
        _memkv_kernel,
        grid=(n // tm,),
        in_specs=[pl.BlockSpec((tm, D_MODEL), lambda t: (t, 0)), _const_spec(wk.shape),
                  _const_spec(wv.shape)],
        out_specs=[out, out],
        out_shape=[jax.ShapeDtypeStruct((n, X_WIDTH), F32)] * 2,
        compiler_params=_params(1),
        name="memkv",
    )(mem2d, wk, wv)


def _xattn_kernel(x_ref, wqx_ref, mk_ref, mv_ref, y_ref, *, groups, g):
    qx = _dot(x_ref[...].astype(BF16), wqx_ref[...])
    units = [(s, h) for s in range(groups) for h in range(X_HEADS)]
    rows = lambda s: slice(s * g, (s + 1) * g)
    cols = lambda h: slice(h * X_HEAD_DIM, (h + 1) * X_HEAD_DIM)

    def head_of(ref, s, h):
        return ref[s, pl.ds(h, MEM_LEN, stride=X_HEADS), :].astype(BF16)

    scores = [_dot_nt(qx[rows(s), cols(h)].astype(BF16), head_of(mk_ref, s, h)) for s, h in units]
    weights = []
    for sc in scores:
        sc = sc * X_SCALE
        p = jnp.exp(sc - jnp.max(sc, axis=-1, keepdims=True))
        weights.append((p / jnp.sum(p, axis=-1, keepdims=True)).astype(BF16))
    for (s, h), p in zip(units, weights):
        y_ref[rows(s), cols(h)] = _dot(p, head_of(mv_ref, s, h))


def _xattn(x2d, wqx, mem_k, mem_v, groups, g):
    ntok = x2d.shape[0]
    tm = groups * g
    mem = pl.BlockSpec((groups, MEM_LEN * X_HEADS, X_HEAD_DIM), lambda t: (t, 0, 0))
    return pl.pallas_call(
        functools.partial(_xattn_kernel, groups=groups, g=g),
        grid=(ntok // tm,),
        in_specs=[pl.BlockSpec((tm, D_MODEL), lambda t: (t, 0)), _const_spec(wqx.shape), mem, mem],
        out_specs=pl.BlockSpec((tm, X_WIDTH), lambda t: (t, 0)),
        out_shape=jax.ShapeDtypeStruct((ntok, X_WIDTH), F32),
        compiler_params=_params(1),
        name="xattn",
    )(x2d, wqx, mem_k, mem_v)


def _merge_kernel(x_ref, ymla_ref, *refs, tm, seq_rows, tiles_per_seq, alpha):
    one_seq = tiles_per_seq is not None
    if one_seq:
        mk_ref, mv_ref, wqx_ref = refs[:3]
    else:
        yx_ref, prev_ref = refs[:2]
    (wb_ref, wpm_ref, wpc_ref, wpx_ref, wo_ref, convw_ref, lng_ref, lnb_ref,
     out_ref, pre_ref, carry_scr) = refs[3 if one_seq else 2:]
    x = x_ref[...]
    xb = x.astype(BF16)
    heads = [slice(h * X_HEAD_DIM, (h + 1) * X_HEAD_DIM) for h in range(X_HEADS)]
    if one_seq:
        qx = _dot(xb, wqx_ref[...])
        x_scores = [_dot_nt(qx[:, hs].astype(BF16), mk_ref[0, :, hs].astype(BF16)) for hs in heads]

    def proj(k):
        lo = k * CONV_DIM if k < 6 else 6 * CONV_DIM + (k - 6) * D_MODEL
        hi = lo + (CONV_DIM if k < 6 else D_MODEL)
        return _dot(xb, wb_ref[:, lo:hi])

    def branch(y, z, gate, wp_ref):
        return jax.nn.sigmoid(gate) * _dot((y * jax.nn.silu(z)).astype(BF16), wp_ref[...])

    m = branch(ymla_ref[...], proj(0), proj(6), wpm_ref)

    pre = proj(2) * proj(3)
    row = lax.broadcasted_iota(jnp.int32, (tm, CONV_DIM), 0)
    if one_seq:
        @pl.when(pl.program_id(0) % tiles_per_seq == 0)
        def _():
            carry_scr[...] = jnp.zeros(carry_scr.shape, F32)
        prev1 = jnp.broadcast_to(carry_scr[7:8, :], (tm, CONV_DIM))
        prev0 = jnp.broadcast_to(carry_scr[6:7, :], (tm, CONV_DIM))
        pos = row
    else:
        n_seq = tm // seq_rows
        prev = prev_ref[...]
        bc = lambda r: jnp.broadcast_to(prev[:, r:r + 1, :], (n_seq, seq_rows, CONV_DIM)).reshape(tm, CONV_DIM)
        prev0, prev1 = bc(0), bc(1)
        pos = _mod_pow2(row, seq_rows)
    back1 = jnp.where(pos == 0, prev1, pltpu.roll(pre, 1, 0))
    back2 = jnp.where(pos == 0, prev0, jnp.where(pos == 1, prev1, pltpu.roll(pre, 2, 0)))
    cw = convw_ref[...]
    conv = cw[0:1, :] * back2 + cw[1:2, :] * back1 + cw[2:3, :] * pre
    if one_seq:
        pre_ref[...] = pre[tm - 8:, :]
        carry_scr[...] = pre[tm - 8:, :]
    else:
        pre_ref[...] = pre
    m = m + branch(proj(1) * conv, proj(4), proj(7), wpc_ref)

    if one_seq:
        outs = []
        for hs, sc in zip(heads, x_scores):
            sc = sc * X_SCALE
            p = jnp.exp(sc - jnp.max(sc, axis=-1, keepdims=True))
            p = (p / jnp.sum(p, axis=-1, keepdims=True)).astype(BF16)
            outs.append(_dot(p, mv_ref[0, :, hs].astype(BF16)))
        yx = jnp.concatenate(outs, axis=-1)
    else:
        yx = yx_ref[...]
    m = m + branch(yx, proj(5), proj(8), wpx_ref)

    r = alpha * x + _dot(m.astype(BF16), wo_ref[...])
    mu = jnp.mean(r, axis=-1, keepdims=True)
    d = r - mu
    var = jnp.mean(d * d, axis=-1, keepdims=True)
    out_ref[...] = d * lax.rsqrt(var + NORM_EPS) * lng_ref[...] + lnb_ref[...]


def _merge(x2d, ymla, extra, w, tm, seq_rows, tiles_per_seq, alpha):
    ntok = x2d.shape[0]
    tok = lambda width: pl.BlockSpec((tm, width), lambda t: (t, 0))
    weights = [w["wb"], w["wpm"], w["wpc"], w["wpx"], w["wo"], w["convw"], w["lng"], w["lnb"]]
    if tiles_per_seq is None:
        pre_spec, pre_rows = tok(CONV_DIM), ntok
        extra_specs = [tok(X_WIDTH),
                       pl.BlockSpec((tm // seq_rows, CONV_WIDTH - 1, CONV_DIM), lambda t: (t, 0, 0))]
    else:
        pre_spec = pl.BlockSpec((8, CONV_DIM), lambda t: (t // tiles_per_seq, 0))
        pre_rows = 8 * (ntok // tm // tiles_per_seq)
        mem = pl.BlockSpec((1, MEM_LEN, X_WIDTH), lambda t: (t // tiles_per_seq, 0, 0))
        extra = (*extra, w["wqx"])
        extra_specs = [mem, mem, _const_spec(w["wqx"].shape)]
    return pl.pallas_call(
        functools.partial(_merge_kernel, tm=tm, seq_rows=seq_rows, tiles_per_seq=tiles_per_seq,
                          alpha=alpha),
        grid=(ntok // tm,),
        in_specs=[tok(D_MODEL), tok(MLA_WIDTH)] + extra_specs
        + [_const_spec(a.shape) for a in weights],
        out_specs=[tok(D_MODEL), pre_spec],
        out_shape=[jax.ShapeDtypeStruct((ntok, D_MODEL), F32),
                   jax.ShapeDtypeStruct((pre_rows, CONV_DIM), F32)],
        scratch_shapes=[pltpu.VMEM((8, CONV_DIM), F32)],
        compiler_params=_params(1),
        name="merge",
    )(x2d, ymla, *extra, *weights)


def _swap_halves(w):
    half = ROPE_DIM // 2
    return jnp.concatenate([w[..., half:], w[..., :half]], axis=-1)


def _place_rope(w):
    z = jnp.zeros_like(w)
    return jnp.concatenate([w, z, w, z], axis=-1)


def _pack_weights(w_in, q_norm_g, kv_norm_g, w_uq, w_uk, w_uv, conv_w, w_mk, w_mv,
                  w_p_mla, w_p_conv, w_p_x, w_o, ln_g, ln_b):
    (w_cq, w_ckv, w_kr, w_zmla, w_cb, w_cc, w_ch, w_zc, w_qx, w_zx, w_g) = jnp.split(
        w_in, SPLIT_POINTS, axis=-1)
    tile_heads = lambda a: jnp.tile(a, (1, MLA_HEADS))
    uq = w_uq.reshape(Q_LORA, MLA_HEADS, NOPE_DIM + ROPE_DIM)
    uq_nope, uq_rope = uq[:, :, :NOPE_DIM], uq[:, :, NOPE_DIM:]
    uk_t = jnp.transpose(w_uk, (1, 2, 0))
    uv_t = jnp.transpose(w_uv, (1, 0, 2))
    wa = jnp.concatenate([w_cq, w_ckv, tile_heads(w_kr), tile_heads(_swap_halves(w_kr))], axis=1)
    wuq = jnp.concatenate([uq_nope.reshape(Q_LORA, -1), uq_rope.reshape(Q_LORA, -1),
                           _swap_halves(uq_rope).reshape(Q_LORA, -1)], axis=1)
    zeros = jnp.zeros((MLA_HEADS // 2, NOPE_DIM, KV_LORA), w_uk.dtype)
    wuk = jnp.concatenate([jnp.concatenate([uk_t[0::2], zeros], axis=2),
                           jnp.concatenate([zeros, uk_t[1::2]], axis=2)], axis=1)
    wuv = jnp.stack([jnp.pad(uv_t[h], ((0, 0), (h * V_DIM, MLA_WIDTH - (h + 1) * V_DIM)))
                     for h in range(MLA_HEADS)])
    wa_p = jnp.concatenate([w_cq, w_ckv, _place_rope(w_kr), _place_rope(_swap_halves(w_kr))], axis=1)
    pad_q = jnp.zeros((Q_LORA, MLA_HEADS, HEAD_LANES - NOPE_DIM - ROPE_DIM), w_uq.dtype)
    wuq_p = jnp.concatenate([
        jnp.concatenate([uq_nope, uq_rope, pad_q], axis=-1).reshape(Q_LORA, PACKED),
        jnp.concatenate([jnp.zeros_like(uq_nope), _swap_halves(uq_rope), pad_q],
                        axis=-1).reshape(Q_LORA, PACKED)], axis=1)
    wukt_p = jnp.concatenate([uk_t, jnp.zeros((MLA_HEADS, HEAD_LANES - NOPE_DIM, KV_LORA), w_uk.dtype)],
                             axis=1).reshape(PACKED, KV_LORA)
    wuv_p = w_uv.reshape(KV_LORA, MLA_WIDTH)
    wb = jnp.concatenate([w_zmla, w_cb, w_cc, w_ch, w_zc, w_zx, w_g], axis=1)
    bf = lambda a: a.astype(BF16)
    row = lambda a: a.reshape(1, -1).astype(F32)
    return dict(wa=bf(wa), gq=row(q_norm_g), gkv=row(kv_norm_g), wuq=bf(wuq), wuk=bf(wuk),
                wuv=bf(wuv), wa_p=bf(wa_p), wuq_p=bf(wuq_p), wukt_p=bf(wukt_p), wuv_p=bf(wuv_p),
                wqx=bf(w_qx), wmk=bf(w_mk), wmv=bf(w_mv), wb=bf(wb),
                wpm=bf(w_p_mla), wpc=bf(w_p_conv), wpx=bf(w_p_x), wo=bf(w_o),
                convw=conv_w.astype(F32), lng=row(ln_g), lnb=row(ln_b))


def _rope_cos_sin(pos):
    half = ROPE_DIM // 2
    inv_freq = ROPE_THETA ** (-jnp.arange(half, dtype=jnp.float32) * (2.0 / ROPE_DIM))
    ang = pos.astype(jnp.float32)[:, None] * inv_freq[None, :]
    cos, sin = jnp.cos(ang), jnp.sin(ang)
    return jnp.concatenate([cos, cos], axis=-1), jnp.concatenate([-sin, sin], axis=-1)


def _prompt_tables(pos):
    cos, sin = _rope_cos_sin(pos)
    n = pos.shape[0]
    ones = jnp.ones((n, NOPE_DIM), F32)
    z32 = jnp.zeros((n, ROPE_DIM), F32)
    z64 = jnp.zeros((n, NOPE_DIM), F32)
    cq = jnp.concatenate([ones, cos, z32], axis=-1) * MLA_EXP2_SCALE
    sq = jnp.concatenate([z64, sin, z32], axis=-1) * MLA_EXP2_SCALE
    return cq, sq, _place_rope(cos), _place_rope(sin)


def kernel(x_prompt, x_sample, mem_prompt, cache_ckv, cache_kr, cache_conv, cache_mem_k,
           cache_mem_v, page_table, w_in, q_norm_g, kv_norm_g, w_uq, w_uk, w_uv, conv_w,
           w_mk, w_mv, w_p_mla, w_p_conv, w_p_x, w_o, ln_g, ln_b):
    depth = w_in.shape[0]
    batch, seq, _ = x_prompt.shape
    dec_batch, dec_seq, _ = x_sample.shape
    past_len = page_table.shape[1] * PAGE_SIZE
    alpha = (2 * depth) ** 0.25
    tm_p, tq, tm_s, x_groups_s, cpg = 512, KT_TILE, 256, 8, 32

    tables_p = _prompt_tables(jnp.arange(seq, dtype=jnp.float32))
    cos_s, sin_s = _rope_cos_sin(past_len + jnp.arange(dec_seq, dtype=jnp.float32))
    cos_s = jnp.tile(cos_s, (tm_s // dec_seq, MLA_HEADS))
    sin_s = jnp.tile(sin_s, (tm_s // dec_seq, MLA_HEADS))

    hp = x_prompt.reshape(batch * seq, D_MODEL)
    hs = x_sample.reshape(dec_batch * dec_seq, D_MODEL)
    mem_shape = (MEM_LEN, X_HEADS, X_HEAD_DIM)
    outs = [[] for _ in range(8)]
    for l in range(depth):
        w = _pack_weights(w_in[l], q_norm_g[l], kv_norm_g[l], w_uq[l], w_uk[l], w_uv[l], conv_w[l],
                          w_mk[l], w_mv[l], w_p_mla[l], w_p_conv[l], w_p_x[l], w_o[l], ln_g[l], ln_b[l])
        qp, kt, v, ckv_p, kr_p = _qkv_prompt(hp, w, tables_p, tm_p, seq // tm_p)
        ymla = _mla_prompt(qp, kt, v, batch, seq, tq)
        memk, memv = _memkv(mem_prompt.reshape(batch * MEM_LEN, D_MODEL), w["wmk"], w["wmv"], MEM_LEN)
        mem_kv = (memk.reshape(batch, MEM_LEN, X_WIDTH), memv.reshape(batch, MEM_LEN, X_WIDTH))
        hp, pre_p = _merge(hp, ymla, mem_kv, w, tm_p, seq, seq // tm_p, alpha)
        memk = memk.reshape(batch, *mem_shape)
        memv = memv.reshape(batch, *mem_shape)
        qlat, qrope, ckv_s, kr_s = _qkv_sample(hs, w, cos_s, sin_s, tm_s)
        o_lat = _mla_decode(page_table, qlat, qrope, ckv_s, kr_s, cache_ckv[l],
                            jnp.swapaxes(cache_kr[l], 1, 2), dec_seq, cpg)
        ymla = _uv(o_lat, w["wuv"], tm_s)
        interleaved = (dec_batch, MEM_LEN * X_HEADS, X_HEAD_DIM)
        yx = _xattn(hs, w["wqx"], cache_mem_k[l].reshape(interleaved),
                    cache_mem_v[l].reshape(interleaved), x_groups_s, dec_seq)
        hs, pre_s = _merge(hs, ymla, (yx, cache_conv[l]), w, tm_s, dec_seq, None, alpha)

        keep = CONV_WIDTH - 1
        for lst, val in zip(outs, (
                ckv_p.reshape(batch, seq, KV_LORA), kr_p.reshape(batch, seq, ROPE_DIM),
                pre_p.reshape(batch, 8, CONV_DIM)[:, 8 - keep:],
                memk, memv,
                ckv_s.reshape(dec_batch, dec_seq, KV_LORA), kr_s.reshape(dec_batch, dec_seq, ROPE_DIM),
                pre_s.reshape(dec_batch, dec_seq, CONV_DIM)[:, dec_seq - keep:])):
            lst.append(val)
    return (hp.reshape(batch, seq, D_MODEL), hs.reshape(dec_batch, dec_seq, D_MODEL),
            *[jnp.stack(lst) for lst in outs])
```

```python
import functools
import math

import jax
import jax.numpy as jnp
import numpy as np
from jax import lax
from jax.experimental import pallas as pl
from jax.experimental.pallas import tpu as pltpu

D_MODEL = 1024
PAGE_SIZE = 128
MLA_HEADS = 8
NOPE_DIM = 64
ROPE_DIM = 32
V_DIM = 64
Q_LORA = 384
KV_LORA = 256
MLA_WIDTH = MLA_HEADS * V_DIM
MLA_SCALE = (NOPE_DIM + ROPE_DIM) ** -0.5
ROPE_THETA = 10000.0
CONV_DIM = D_MODEL // 2
CONV_WIDTH = 3
MEM_LEN = 256
X_HEADS = 4
X_HEAD_DIM = 128
X_WIDTH = X_HEADS * X_HEAD_DIM
X_SCALE = X_HEAD_DIM ** -0.5
N_BRANCH = 3
NORM_EPS = 1e-6
NEG_INF = -1e30
SPLIT_SIZES = (Q_LORA, KV_LORA, ROPE_DIM, MLA_WIDTH,
               CONV_DIM, CONV_DIM, CONV_DIM, CONV_DIM,
               X_WIDTH, X_WIDTH, N_BRANCH * D_MODEL)
SPLIT_POINTS = tuple(int(v) for v in np.cumsum(SPLIT_SIZES)[:-1])

LANES = 128
ROPE_LANES = MLA_HEADS * ROPE_DIM
HEAD_LANES = LANES
PACKED = MLA_HEADS * HEAD_LANES
KT_TILE = 256
FETCH_AHEAD = 2
N_PAGE_SLOTS = FETCH_AHEAD + 2
VMEM_LIMIT = 48 * 1024 * 1024
BF16 = jnp.bfloat16
F32 = jnp.float32
NT_DIMS = (((1,), (1,)), ((), ()))
MLA_EXP2_SCALE = MLA_SCALE * math.log2(math.e)


def _dot(a, b):
    return jnp.dot(a, b, preferred_element_type=F32)


def _dot_nt(a, b):
    return lax.dot_general(a, b, NT_DIMS, preferred_element_type=F32)


def _rms(x, g):
    return x * lax.rsqrt(jnp.mean(x * x, axis=-1, keepdims=True) + NORM_EPS) * g


def _const_spec(shape):
    return pl.BlockSpec(shape, lambda *_: (0,) * len(shape), pipeline_mode=pl.Buffered(1))


def _mod_pow2(x, n):
    assert n & (n - 1) == 0
    return x & (n - 1)


def _params(n_axes):
    return pltpu.CompilerParams(dimension_semantics=("arbitrary",) * n_axes,
                                vmem_limit_bytes=VMEM_LIMIT)


def _qkv_prompt_kernel(x_ref, wa_ref, gq_ref, gkv_ref, wuq_ref, wukt_ref, wuv_ref,
                       cq_ref, sq_ref, ck_ref, sk_ref,
                       qp_ref, kt_ref, v_ref, ckv_ref, kr_ref):
    xb = x_ref[...].astype(BF16)
    pa = _dot(xb, wa_ref[...])
    cqn = _rms(pa[:, :Q_LORA], gq_ref[...])
    qa = _dot(cqn.astype(BF16), wuq_ref[...])
    cq = cq_ref[...]
    sq = sq_ref[...]
    for h in range(MLA_HEADS):
        lo = h * HEAD_LANES
        qp_ref[h] = (qa[:, lo:lo + HEAD_LANES] * cq
                     + qa[:, PACKED + lo:PACKED + lo + HEAD_LANES] * sq).astype(BF16)
    ckvn = _rms(pa[:, Q_LORA:Q_LORA + KV_LORA], gkv_ref[...])
    ckv_ref[...] = ckvn
    vp = _dot(ckvn.astype(BF16), wuv_ref[...])
    for pair in range(MLA_HEADS // 2):
        v_ref[pair] = vp[:, pair * HEAD_LANES:(pair + 1) * HEAD_LANES].astype(BF16)
    k0 = Q_LORA + KV_LORA
    krot = pa[:, k0:k0 + HEAD_LANES] * ck_ref[...] + pa[:, k0 + HEAD_LANES:] * sk_ref[...]
    lane = lax.broadcasted_iota(jnp.int32, krot.shape, 1)
    k_rope = jnp.where(lane >= NOPE_DIM, krot, 0.0)
    for t in range(kt_ref.shape[0]):
        rows = slice(t * KT_TILE, (t + 1) * KT_TILE)
        k_nope_t = _dot(wukt_ref[...], ckvn[rows, :].T.astype(BF16))
        k_rope_t = k_rope[rows, :].T
        kr_ref[0, :, rows] = k_rope_t[NOPE_DIM:NOPE_DIM + ROPE_DIM, :]
        for h in range(MLA_HEADS):
            kt_ref[t, h] = (k_nope_t[h * HEAD_LANES:(h + 1) * HEAD_LANES, :] + k_rope_t).astype(BF16)


def _qkv_prompt(x2d, w, tables, tm, table_tiles):
    ntok = x2d.shape[0]
    tok = lambda width: pl.BlockSpec((tm, width), lambda t: (t, 0))
    tab = pl.BlockSpec((tm, HEAD_LANES), lambda t: (t % table_tiles, 0))
    head_tok = pl.BlockSpec((MLA_HEADS, tm, HEAD_LANES), lambda t: (0, t, 0))
    weights = [w["wa_p"], w["gq"], w["gkv"], w["wuq_p"], w["wukt_p"], w["wuv_p"]]
    return pl.pallas_call(
        _qkv_prompt_kernel,
        grid=(ntok // tm,),
        in_specs=[tok(D_MODEL)] + [_const_spec(a.shape) for a in weights] + [tab] * 4,
        out_specs=[head_tok,
                   pl.BlockSpec((tm // KT_TILE, MLA_HEADS, HEAD_LANES, KT_TILE), lambda t: (t, 0, 0, 0)),
                   pl.BlockSpec((MLA_HEADS // 2, tm, HEAD_LANES), lambda t: (0, t, 0)),
                   tok(KV_LORA),
                   pl.BlockSpec((1, ROPE_DIM, tm), lambda t: (t // table_tiles, 0, t % table_tiles))],
        out_shape=[jax.ShapeDtypeStruct((MLA_HEADS, ntok, HEAD_LANES), BF16),
                   jax.ShapeDtypeStruct((ntok // KT_TILE, MLA_HEADS, HEAD_LANES, KT_TILE), BF16),
                   jax.ShapeDtypeStruct((MLA_HEADS // 2, ntok, HEAD_LANES), BF16),
                   jax.ShapeDtypeStruct((ntok, KV_LORA), F32),
                   jax.ShapeDtypeStruct((ntok // (tm * table_tiles), ROPE_DIM, tm * table_tiles), F32)],
        compiler_params=_params(1),
        name="qkv_prompt",
    )(x2d, *weights, *tables)


def _qkv_sample_kernel(x_ref, wa_ref, gq_ref, gkv_ref, wuq_ref, wuk_ref, cos_ref, sin_ref,
                       qlat_ref, qrope_ref, ckv_ref, kr_ref):
    cos = cos_ref[...]
    sin = sin_ref[...]
    xb = x_ref[...].astype(BF16)
    pa = _dot(xb, wa_ref[...])
    cqn = _rms(pa[:, :Q_LORA], gq_ref[...])
    qa = _dot(cqn.astype(BF16), wuq_ref[...])
    r0 = MLA_HEADS * NOPE_DIM
    qrope_ref[...] = qa[:, r0:r0 + ROPE_LANES] * cos + qa[:, r0 + ROPE_LANES:] * sin
    for p in range(MLA_HEADS // 2):
        ql = _dot(qa[:, 128 * p:128 * (p + 1)].astype(BF16), wuk_ref[p])
        qlat_ref[2 * p] = ql[:, :KV_LORA]
        qlat_ref[2 * p + 1] = ql[:, KV_LORA:]
    ckv_ref[...] = _rms(pa[:, Q_LORA:Q_LORA + KV_LORA], gkv_ref[...])
    k0 = Q_LORA + KV_LORA
    krt = pa[:, k0:k0 + ROPE_LANES] * cos + pa[:, k0 + ROPE_LANES:] * sin
    kr_ref[...] = krt[:, :ROPE_DIM]


def _qkv_sample(x2d, w, cos, sin, tm):
    ntok = x2d.shape[0]
    tok = lambda width: pl.BlockSpec((tm, width), lambda t: (t, 0))
    tab = pl.BlockSpec((tm, ROPE_LANES), lambda t: (0, 0))
    weights = [w["wa"], w["gq"], w["gkv"], w["wuq"], w["wuk"]]
    return pl.pallas_call(
        _qkv_sample_kernel,
        grid=(ntok // tm,),
        in_specs=[tok(D_MODEL)] + [_const_spec(a.shape) for a in weights] + [tab, tab],
        out_specs=[pl.BlockSpec((MLA_HEADS, tm, KV_LORA), lambda t: (0, t, 0)),
                   tok(ROPE_LANES), tok(KV_LORA), tok(ROPE_DIM)],
        out_shape=[jax.ShapeDtypeStruct((MLA_HEADS, ntok, KV_LORA), F32),
                   jax.ShapeDtypeStruct((ntok, ROPE_LANES), F32),
                   jax.ShapeDtypeStruct((ntok, KV_LORA), F32),
                   jax.ShapeDtypeStruct((ntok, ROPE_DIM), F32)],
        compiler_params=_params(1),
        name="qkv_sample",
    )(x2d, *weights, cos, sin)


def _lane_repeat(x, width):
    return jnp.concatenate([x] * (width // LANES), axis=-1)


def _softmax_weights(s, m_old, l_old, scale):
    mul = (lambda x: x) if scale == 1.0 else (lambda x: x * scale)
    m_new = jnp.maximum(m_old, jnp.max(s, axis=-1, keepdims=True))
    alpha = jnp.exp2(mul(m_old - m_new))
    p = jnp.exp2(mul(s - _lane_repeat(m_new, s.shape[-1])))
    l_new = alpha * l_old + jnp.sum(p, axis=-1, keepdims=True)
    return m_new, l_new, alpha, p.astype(BF16)


def _softmax_init(m_scr, l_scr, acc_scr):
    m_scr[...] = jnp.full(m_scr.shape, -jnp.inf, F32)
    l_scr[...] = jnp.zeros(l_scr.shape, F32)
    acc_scr[...] = jnp.zeros(acc_scr.shape, F32)


def _mla_prompt_kernel(qp_ref, kt_ref, v_ref, y_ref, m_scr, l_scr, acc_scr, *, tq, rq):
    assert tq == KT_TILE and tq % rq == 0
    i = pl.program_id(1)
    _softmax_init(m_scr, l_scr, acc_scr)
    chunks = [(h, r0) for h in range(MLA_HEADS) for r0 in range(0, tq, rq)]
    first_of_pair = lax.broadcasted_iota(jnp.int32, (rq, HEAD_LANES), 1) < V_DIM

    def score(j, n_tiles):
        return [jnp.concatenate([_dot(qp_ref[h, r0:r0 + rq, :], kt_ref[j + t, h])
                                 for t in range(n_tiles)], axis=1) for h, r0 in chunks]

    def softmax(scores, diagonal):
        weights = {}
        for (h, r0), s in zip(chunks, scores):
            rs = slice(r0, r0 + rq)
            if diagonal:
                q_pos = r0 + lax.broadcasted_iota(jnp.int32, (rq, tq), 0)
                k_pos = lax.broadcasted_iota(jnp.int32, (rq, tq), 1)
                s = jnp.where(k_pos <= q_pos, s, NEG_INF)
            m_scr[h, rs, :], l_scr[h, rs, :], alpha, p = _softmax_weights(
                s, m_scr[h, rs, :], l_scr[h, rs, :], 1.0)
            weights[h, r0] = (alpha, p)
        return weights

    def weigh(j, n_tiles, weights):
        keys = pl.ds(pl.multiple_of(j * tq, tq), n_tiles * tq)
        for pair in range(MLA_HEADS // 2):
            ha, hb = 2 * pair, 2 * pair + 1
            for r0 in range(0, tq, rq):
                rs = slice(r0, r0 + rq)
                (alpha_a, p_a), (alpha_b, p_b) = weights[ha, r0], weights[hb, r0]
                v_pair = v_ref[pair, keys, :]
                pv = jnp.where(first_of_pair, _dot(p_a, v_pair), _dot(p_b, v_pair))
                acc_scr[pair, rs, :] = (jnp.where(first_of_pair, alpha_a, alpha_b)
                                        * acc_scr[pair, rs, :] + pv)

    def step(j, n_tiles, diagonal):
        weigh(j, n_tiles, softmax(score(j, n_tiles), diagonal))

    def body(j4, carry):
        step(4 * j4, 4, False)
        return carry

    n_quads = lax.shift_right_logical(i, 2)
    lax.fori_loop(0, n_quads, body, 0)

    @pl.when((i & 2) != 0)
    def _():
        step(4 * n_quads, 2, False)

    @pl.when((i & 1) != 0)
    def _():
        step(i - 1, 1, False)

    step(i, 1, True)
    first = lax.broadcasted_iota(jnp.int32, (tq, HEAD_LANES), 1) < V_DIM
    for pair in range(MLA_HEADS // 2):
        inv_l = jnp.where(first, 1.0 / l_scr[2 * pair], 1.0 / l_scr[2 * pair + 1])
        y_ref[:, pair * HEAD_LANES:(pair + 1) * HEAD_LANES] = acc_scr[pair] * inv_l


def _mla_prompt(qp, kt, v, batch, seq, tq):
    ntok = batch * seq
    nq = seq // tq
    return pl.pallas_call(
        functools.partial(_mla_prompt_kernel, tq=tq, rq=256),
        grid=(batch, nq),
        in_specs=[pl.BlockSpec((MLA_HEADS, tq, HEAD_LANES), lambda b, i: (0, b * nq + i, 0)),
                  pl.BlockSpec((seq // KT_TILE, MLA_HEADS, HEAD_LANES, KT_TILE),
                               lambda b, i: (b, 0, 0, 0)),
                  pl.BlockSpec((MLA_HEADS // 2, seq, HEAD_LANES), lambda b, i: (0, b, 0))],
        out_specs=pl.BlockSpec((tq, MLA_WIDTH), lambda b, i: (b * nq + i, 0)),
        out_shape=jax.ShapeDtypeStruct((ntok, MLA_WIDTH), F32),
        scratch_shapes=[pltpu.VMEM((MLA_HEADS, tq, LANES), F32),
                        pltpu.VMEM((MLA_HEADS, tq, LANES), F32),
                        pltpu.VMEM((MLA_HEADS // 2, tq, HEAD_LANES), F32)],
        compiler_params=_params(2),
        name="mla_prompt",
    )(qp, kt, v)


def _mla_decode_kernel(pt_ref, qlat_ref, qrope_ref, ckvn_ref, krn_ref, ckv_hbm, krt_hbm, o_ref,
                       ckv_buf, krt_buf, sem, ql_scr, qr_scr, kn_scr, krn_scr, s_scr,
                       m_scr, l_scr, acc_scr, *, cpg, n_chunks, dec_seq):
    assert n_chunks % N_PAGE_SLOTS == 0 and N_PAGE_SLOTS == FETCH_AHEAD + 2
    seq = pl.program_id(0)
    n_seq = pl.num_programs(0)
    rows = MLA_HEADS * dec_seq

    def slot_of(c):
        return _mod_pow2(c, N_PAGE_SLOTS)

    def page_copies(sq, chunk, slot):
        copies = []
        for k in range(cpg):
            page = pt_ref[sq, chunk * cpg + k]
            copies.append(pltpu.make_async_copy(ckv_hbm.at[page], ckv_buf.at[slot, k], sem.at[slot, 0]))
            lanes = pl.ds(k * PAGE_SIZE, PAGE_SIZE)
            copies.append(pltpu.make_async_copy(krt_hbm.at[page], krt_buf.at[slot, :, lanes], sem.at[slot, 1]))
        return copies

    def start_chunk(sq, chunk, slot):
        for n, cp in enumerate(page_copies(sq, chunk, slot)):
            cp.start(priority=(n // 2) % 2)

    def chunk_ahead(c):
        ahead = c + FETCH_AHEAD
        spills = ahead >= n_chunks
        more_seqs = seq + 1 < n_seq
        sq = jnp.where(spills & more_seqs, seq + 1, seq)
        ch = jnp.where(spills, jnp.where(more_seqs, ahead - n_chunks, c), ahead)
        return sq, ch

    def fetch_ahead_and_wait(c):
        start_chunk(*chunk_ahead(c), slot_of(c + FETCH_AHEAD))
        for cp in page_copies(seq, c, slot_of(c)):
            cp.wait()

    def score(slot):
        ql = ql_scr[...]
        rope = _dot(qr_scr[...], krt_buf[slot].astype(BF16))
        for k in range(cpg):
            keys = slice(k * PAGE_SIZE, (k + 1) * PAGE_SIZE)
            s_scr[:, keys] = _dot_nt(ql, ckv_buf[slot, k].astype(BF16)) + rope[:, keys]

    chunk_keys = cpg * PAGE_SIZE

    def softmax():
        m_scr[...], l_scr[...], alpha, p = _softmax_weights(
            s_scr[:, :chunk_keys], m_scr[...], l_scr[...], MLA_EXP2_SCALE)
        return alpha, p

    def weighted(slot, p):
        pv = _dot(p[:, :PAGE_SIZE], ckv_buf[slot, 0].astype(BF16))
        for k in range(1, cpg):
            pv = pv + _dot(p[:, PAGE_SIZE * k:PAGE_SIZE * (k + 1)], ckv_buf[slot, k].astype(BF16))
        return pv

    def weigh(slot, alpha, p):
        acc_scr[...] = _lane_repeat(alpha, KV_LORA) * acc_scr[...] + weighted(slot, p)

    @pl.when(seq == 0)
    def _():
        for c in range(FETCH_AHEAD):
            start_chunk(0, c, c)

    ql_scr[...] = jnp.concatenate([qlat_ref[h] for h in range(MLA_HEADS)], axis=0).astype(BF16)
    qr = qrope_ref[...]
    qr_scr[...] = jnp.concatenate(
        [qr[:, ROPE_DIM * h:ROPE_DIM * (h + 1)] for h in range(MLA_HEADS)], axis=0).astype(BF16)
    kn_scr[...] = jnp.zeros(kn_scr.shape, F32)
    krn_scr[...] = jnp.zeros(krn_scr.shape, F32)
    kn_scr[:dec_seq, :] = ckvn_ref[...]
    krn_scr[:dec_seq, :] = krn_ref[...]
    _softmax_init(m_scr, l_scr, acc_scr)
    ck_new = kn_scr[...].astype(BF16)
    s_new = _dot_nt(ql_scr[...], ck_new) + _dot_nt(qr_scr[...], krn_scr[...].astype(BF16))
    q_pos = _mod_pow2(lax.broadcasted_iota(jnp.int32, (rows, PAGE_SIZE), 0), dec_seq)
    k_pos = lax.broadcasted_iota(jnp.int32, (rows, PAGE_SIZE), 1)
    s_scr[:, chunk_keys:] = jnp.where(k_pos <= q_pos, s_new, NEG_INF)

    fetch_ahead_and_wait(0)
    score(0)

    def chunk_body(c, carry):
        fetch_ahead_and_wait(c)
        alpha, p = softmax()
        score(slot_of(c))
        weigh(slot_of(c - 1), alpha, p)
        return carry

    lax.fori_loop(1, n_chunks, chunk_body, 0)
    _, l_fin, alpha, p = _softmax_weights(s_scr[...], m_scr[...], l_scr[...], MLA_EXP2_SCALE)
    pv = weighted(slot_of(n_chunks - 1), p) + _dot(p[:, chunk_keys:], ck_new)
    o = (_lane_repeat(alpha, KV_LORA) * acc_scr[...] + pv) * _lane_repeat(1.0 / l_fin, KV_LORA)

    @pl.when(seq == n_seq - 1)
    def _():
        for c in range(n_chunks - FETCH_AHEAD, n_chunks):
            for cp in page_copies(seq, c, slot_of(c + FETCH_AHEAD)):
                cp.wait()

    for h in range(MLA_HEADS):
        o_ref[h] = o[h * dec_seq:(h + 1) * dec_seq, :]


def _mla_decode(page_table, qlat, qrope, ckv_new, kr_new, cache_ckv, cache_krt, dec_seq, cpg):
    n_seq, n_pages = page_table.shape
    ntok = n_seq * dec_seq
    rows = MLA_HEADS * dec_seq
    tok = lambda width: pl.BlockSpec((dec_seq, width), lambda s, pt: (s, 0))
    head_tok = pl.BlockSpec((MLA_HEADS, dec_seq, KV_LORA), lambda s, pt: (0, s, 0))
    hbm = pl.BlockSpec(memory_space=pl.ANY)
    grid_spec = pltpu.PrefetchScalarGridSpec(
        num_scalar_prefetch=1,
        grid=(n_seq,),
        in_specs=[head_tok, tok(ROPE_LANES), tok(KV_LORA), tok(ROPE_DIM), hbm, hbm],
        out_specs=head_tok,
        scratch_shapes=[pltpu.VMEM((N_PAGE_SLOTS, cpg, PAGE_SIZE, KV_LORA), F32),
                        pltpu.VMEM((N_PAGE_SLOTS, ROPE_DIM, cpg * PAGE_SIZE), F32),
                        pltpu.SemaphoreType.DMA((N_PAGE_SLOTS, 2)),
                        pltpu.VMEM((rows, KV_LORA), BF16), pltpu.VMEM((rows, ROPE_DIM), BF16),
                        pltpu.VMEM((PAGE_SIZE, KV_LORA), F32), pltpu.VMEM((PAGE_SIZE, ROPE_DIM), F32),
                        pltpu.VMEM((rows, (cpg + 1) * PAGE_SIZE), F32),
                        pltpu.VMEM((rows, LANES), F32), pltpu.VMEM((rows, LANES), F32),
                        pltpu.VMEM((rows, KV_LORA), F32)],
    )
    return pl.pallas_call(
        functools.partial(_mla_decode_kernel, cpg=cpg, n_chunks=n_pages // cpg, dec_seq=dec_seq),
        grid_spec=grid_spec,
        out_shape=jax.ShapeDtypeStruct((MLA_HEADS, ntok, KV_LORA), F32),
        compiler_params=_params(1),
        name="mla_decode",
    )(page_table, qlat, qrope, ckv_new, kr_new, cache_ckv, cache_krt)


def _uv_kernel(o_ref, wuv_ref, y_ref):
    y = _dot(o_ref[0].astype(BF16), wuv_ref[0])
    for h in range(1, MLA_HEADS):
        y = y + _dot(o_ref[h].astype(BF16), wuv_ref[h])
    y_ref[...] = y


def _uv(o_lat, wuv, tm):
    ntok = o_lat.shape[1]
    return pl.pallas_call(
        _uv_kernel,
        grid=(ntok // tm,),
        in_specs=[pl.BlockSpec((MLA_HEADS, tm, KV_LORA), lambda t: (0, t, 0)), _const_spec(wuv.shape)],
        out_specs=pl.BlockSpec((tm, MLA_WIDTH), lambda t: (t, 0)),
        out_shape=jax.ShapeDtypeStruct((ntok, MLA_WIDTH), F32),
        compiler_params=_params(1),
        name="uv",
    )(o_lat, wuv)


def _memkv_kernel(mem_ref, wk_ref, wv_ref, k_ref, v_ref):
    mb = mem_ref[...].astype(BF16)
    k_ref[...] = _dot(mb, wk_ref[...])
    v_ref[...] = _dot(mb, wv_ref[...])


def _memkv(mem2d, wk, wv, tm):
    n = mem2d.shape[0]
    out = pl.BlockSpec((tm, X_WIDTH), lambda t: (t, 0))
    return pl.pallas_call(
        _memkv_kernel,
        grid=(n // tm,),
        in_specs=[pl.BlockSpec((tm, D_MODEL), lambda t: (t, 0)), _const_spec(wk.shape),
                  _const_spec(wv.shape)],
        out_specs=[out, out],
        out_shape=[jax.ShapeDtypeStruct((n, X_WIDTH), F32)] * 2,
        compiler_params=_params(1),
        name="memkv",
    )(mem2d, wk, wv)


def _xattn_kernel(x_ref, wqx_ref, mk_ref, mv_ref, y_ref, *, groups, g):
    qx = _dot(x_ref[...].astype(BF16), wqx_ref[...])
    units = [(s, h) for s in range(groups) for h in range(X_HEADS)]
    rows = lambda s: slice(s * g, (s + 1) * g)
    cols = lambda h: slice(h * X_HEAD_DIM, (h + 1) * X_HEAD_DIM)

    def head_of(ref, s, h):
        return ref[s, pl.ds(h, MEM_LEN, stride=X_HEADS), :].astype(BF16)

    scores = [_dot_nt(qx[rows(s), cols(h)].astype(BF16), head_of(mk_ref, s, h)) for s, h in units]
    weights = []
    for sc in scores:
        sc = sc * X_SCALE
        p = jnp.exp(sc - jnp.max(sc, axis=-1, keepdims=True))
        weights.append((p / jnp.sum(p, axis=-1, keepdims=True)).astype(BF16))
    for (s, h), p in zip(units, weights):
        y_ref[rows(s), cols(h)] = _dot(p, head_of(mv_ref, s, h))


def _xattn(x2d, wqx, mem_k, mem_v, groups, g):
    ntok = x2d.shape[0]
    tm = groups * g
    mem = pl.BlockSpec((groups, MEM_LEN * X_HEADS, X_HEAD_DIM), lambda t: (t, 0, 0))
    return pl.pallas_call(
        functools.partial(_xattn_kernel, groups=groups, g=g),
        grid=(ntok // tm,),
        in_specs=[pl.BlockSpec((tm, D_MODEL), lambda t: (t, 0)), _const_spec(wqx.shape), mem, mem],
        out_specs=pl.BlockSpec((tm, X_WIDTH), lambda t: (t, 0)),
        out_shape=jax.ShapeDtypeStruct((ntok, X_WIDTH), F32),
        compiler_params=_params(1),
        name="xattn",
    )(x2d, wqx, mem_k, mem_v)


def _merge_kernel(x_ref, ymla_ref, *refs, tm, seq_rows, tiles_per_seq, alpha):
    one_seq = tiles_per_seq is not None
    if one_seq:
        mk_ref, mv_ref, wqx_ref = refs[:3]
    else:
        yx_ref, prev_ref = refs[:2]
    (wb_ref, wpm_ref, wpc_ref, wpx_ref, wo_ref, convw_ref, lng_ref, lnb_ref,
     out_ref, pre_ref, carry_scr) = refs[3 if one_seq else 2:]
    x = x_ref[...]
    xb = x.astype(BF16)
    heads = [slice(h * X_HEAD_DIM, (h + 1) * X_HEAD_DIM) for h in range(X_HEADS)]
    if one_seq:
        qx = _dot(xb, wqx_ref[...])
        x_scores = [_dot_nt(qx[:, hs].astype(BF16), mk_ref[0, :, hs].astype(BF16)) for hs in heads]

    def proj(k):
        lo = k * CONV_DIM if k < 6 else 6 * CONV_DIM + (k - 6) * D_MODEL
        hi = lo + (CONV_DIM if k < 6 else D_MODEL)
        return _dot(xb, wb_ref[:, lo:hi])

    def branch(y, z, gate, wp_ref):
        return jax.nn.sigmoid(gate) * _dot((y * jax.nn.silu(z)).astype(BF16), wp_ref[...])

    m = branch(ymla_ref[...], proj(0), proj(6), wpm_ref)

    pre = proj(2) * proj(3)
    row = lax.broadcasted_iota(jnp.int32, (tm, CONV_DIM), 0)
    if one_seq:
        @pl.when(pl.program_id(0) % tiles_per_seq == 0)
        def _():
            carry_scr[...] = jnp.zeros(carry_scr.shape, F32)
        prev1 = jnp.broadcast_to(carry_scr[7:8, :], (tm, CONV_DIM))
        prev0 = jnp.broadcast_to(carry_scr[6:7, :], (tm, CONV_DIM))
        pos = row
    else:
        n_seq = tm // seq_rows
        prev = prev_ref[...]
        bc = lambda r: jnp.broadcast_to(prev[:, r:r + 1, :], (n_seq, seq_rows, CONV_DIM)).reshape(tm, CONV_DIM)
        prev0, prev1 = bc(0), bc(1)
        pos = _mod_pow2(row, seq_rows)
    back1 = jnp.where(pos == 0, prev1, pltpu.roll(pre, 1, 0))
    back2 = jnp.where(pos == 0, prev0, jnp.where(pos == 1, prev1, pltpu.roll(pre, 2, 0)))
    cw = convw_ref[...]
    conv = cw[0:1, :] * back2 + cw[1:2, :] * back1 + cw[2:3, :] * pre
    if one_seq:
        pre_ref[...] = pre[tm - 8:, :]
        carry_scr[...] = pre[tm - 8:, :]
    else:
        pre_ref[...] = pre
    m = m + branch(proj(1) * conv, proj(4), proj(7), wpc_ref)

    if one_seq:
        outs = []
        for hs, sc in zip(heads, x_scores):
            sc = sc * X_SCALE
            p = jnp.exp(sc - jnp.max(sc, axis=-1, keepdims=True))
            p = (p / jnp.sum(p, axis=-1, keepdims=True)).astype(BF16)
            outs.append(_dot(p, mv_ref[0, :, hs].astype(BF16)))
        yx = jnp.concatenate(outs, axis=-1)
    else:
        yx = yx_ref[...]
    m = m + branch(yx, proj(5), proj(8), wpx_ref)

    r = alpha * x + _dot(m.astype(BF16), wo_ref[...])
    mu = jnp.mean(r, axis=-1, keepdims=True)
    d = r - mu
    var = jnp.mean(d * d, axis=-1, keepdims=True)
    out_ref[...] = d * lax.rsqrt(var + NORM_EPS) * lng_ref[...] + lnb_ref[...]


def _merge(x2d, ymla, extra, w, tm, seq_rows, tiles_per_seq, alpha):
    ntok = x2d.shape[0]
    tok = lambda width: pl.BlockSpec((tm, width), lambda t: (t, 0))
    weights = [w["wb"], w["wpm"], w["wpc"], w["wpx"], w["wo"], w["convw"], w["lng"], w["lnb"]]
    if tiles_per_seq is None:
        pre_spec, pre_rows = tok(CONV_DIM), ntok
        extra_specs = [tok(X_WIDTH),
                       pl.BlockSpec((tm // seq_rows, CONV_WIDTH - 1, CONV_DIM), lambda t: (t, 0, 0))]
    else:
        pre_spec = pl.BlockSpec((8, CONV_DIM), lambda t: (t // tiles_per_seq, 0))
        pre_rows = 8 * (ntok // tm // tiles_per_seq)
        mem = pl.BlockSpec((1, MEM_LEN, X_WIDTH), lambda t: (t // tiles_per_seq, 0, 0))
        extra = (*extra, w["wqx"])
        extra_specs = [mem, mem, _const_spec(w["wqx"].shape)]
    return pl.pallas_call(
        functools.partial(_merge_kernel, tm=tm, seq_rows=seq_rows, tiles_per_seq=tiles_per_seq,
                          alpha=alpha),
        grid=(ntok // tm,),
        in_specs=[tok(D_MODEL), tok(MLA_WIDTH)] + extra_specs
        + [_const_spec(a.shape) for a in weights],
        out_specs=[tok(D_MODEL), pre_spec],
        out_shape=[jax.ShapeDtypeStruct((ntok, D_MODEL), F32),
                   jax.ShapeDtypeStruct((pre_rows, CONV_DIM), F32)],
        scratch_shapes=[pltpu.VMEM((8, CONV_DIM), F32)],
        compiler_params=_params(1),
        name="merge",
    )(x2d, ymla, *extra, *weights)


def _swap_halves(w):
    half = ROPE_DIM // 2
    return jnp.concatenate([w[..., half:], w[..., :half]], axis=-1)


def _place_rope(w):
    z = jnp.zeros_like(w)
    return jnp.concatenate([w, z, w, z], axis=-1)


def _pack_weights(w_in, q_norm_g, kv_norm_g, w_uq, w_uk, w_uv, conv_w, w_mk, w_mv,
                  w_p_mla, w_p_conv, w_p_x, w_o, ln_g, ln_b):
    (w_cq, w_ckv, w_kr, w_zmla, w_cb, w_cc, w_ch, w_zc, w_qx, w_zx, w_g) = jnp.split(
        w_in, SPLIT_POINTS, axis=-1)
    tile_heads = lambda a: jnp.tile(a, (1, MLA_HEADS))
    uq = w_uq.reshape(Q_LORA, MLA_HEADS, NOPE_DIM + ROPE_DIM)
    uq_nope, uq_rope = uq[:, :, :NOPE_DIM], uq[:, :, NOPE_DIM:]
    uk_t = jnp.transpose(w_uk, (1, 2, 0))
    uv_t = jnp.transpose(w_uv, (1, 0, 2))
    wa = jnp.concatenate([w_cq, w_ckv, tile_heads(w_kr), tile_heads(_swap_halves(w_kr))], axis=1)
    wuq = jnp.concatenate([uq_nope.reshape(Q_LORA, -1), uq_rope.reshape(Q_LORA, -1),
                           _swap_halves(uq_rope).reshape(Q_LORA, -1)], axis=1)
    zeros = jnp.zeros((MLA_HEADS // 2, NOPE_DIM, KV_LORA), w_uk.dtype)
    wuk = jnp.concatenate([jnp.concatenate([uk_t[0::2], zeros], axis=2),
                           jnp.concatenate([zeros, uk_t[1::2]], axis=2)], axis=1)
    wuv = jnp.stack([jnp.pad(uv_t[h], ((0, 0), (h * V_DIM, MLA_WIDTH - (h + 1) * V_DIM)))
                     for h in range(MLA_HEADS)])
    wa_p = jnp.concatenate([w_cq, w_ckv, _place_rope(w_kr), _place_rope(_swap_halves(w_kr))], axis=1)
    pad_q = jnp.zeros((Q_LORA, MLA_HEADS, HEAD_LANES - NOPE_DIM - ROPE_DIM), w_uq.dtype)
    wuq_p = jnp.concatenate([
        jnp.concatenate([uq_nope, uq_rope, pad_q], axis=-1).reshape(Q_LORA, PACKED),
        jnp.concatenate([jnp.zeros_like(uq_nope), _swap_halves(uq_rope), pad_q],
                        axis=-1).reshape(Q_LORA, PACKED)], axis=1)
    wukt_p = jnp.concatenate([uk_t, jnp.zeros((MLA_HEADS, HEAD_LANES - NOPE_DIM, KV_LORA), w_uk.dtype)],
                             axis=1).reshape(PACKED, KV_LORA)
    wuv_p = w_uv.reshape(KV_LORA, MLA_WIDTH)
    wb = jnp.concatenate([w_zmla, w_cb, w_cc, w_ch, w_zc, w_zx, w_g], axis=1)
    bf = lambda a: a.astype(BF16)
    row = lambda a: a.reshape(1, -1).astype(F32)
    return dict(wa=bf(wa), gq=row(q_norm_g), gkv=row(kv_norm_g), wuq=bf(wuq), wuk=bf(wuk),
                wuv=bf(wuv), wa_p=bf(wa_p), wuq_p=bf(wuq_p), wukt_p=bf(wukt_p), wuv_p=bf(wuv_p),
                wqx=bf(w_qx), wmk=bf(w_mk), wmv=bf(w_mv), wb=bf(wb),
                wpm=bf(w_p_mla), wpc=bf(w_p_conv), wpx=bf(w_p_x), wo=bf(w_o),
                convw=conv_w.astype(F32), lng=row(ln_g), lnb=row(ln_b))


def _rope_cos_sin(pos):
    half = ROPE_DIM // 2
    inv_freq = ROPE_THETA ** (-jnp.arange(half, dtype=jnp.float32) * (2.0 / ROPE_DIM))
    ang = pos.astype(jnp.float32)[:, None] * inv_freq[None, :]
    cos, sin = jnp.cos(ang), jnp.sin(ang)
    return jnp.concatenate([cos, cos], axis=-1), jnp.concatenate([-sin, sin], axis=-1)


def _prompt_tables(pos):
    cos, sin = _rope_cos_sin(pos)
    n = pos.shape[0]
    ones = jnp.ones((n, NOPE_DIM), F32)
    z32 = jnp.zeros((n, ROPE_DIM), F32)
    z64 = jnp.zeros((n, NOPE_DIM), F32)
    cq = jnp.concatenate([ones, cos, z32], axis=-1) * MLA_EXP2_SCALE
    sq = jnp.concatenate([z64, sin, z32], axis=-1) * MLA_EXP2_SCALE
    return cq, sq, _place_rope(cos), _place_rope(sin)


def kernel(x_prompt, x_sample, mem_prompt, cache_ckv, cache_kr, cache_conv, cache_mem_k,
           cache_mem_v, page_table, w_in, q_norm_g, kv_norm_g, w_uq, w_uk, w_uv, conv_w,
           w_mk, w_mv, w_p_mla, w_p_conv, w_p_x, w_o, ln_g, ln_b):
    depth = w_in.shape[0]
    batch, seq, _ = x_prompt.shape
    dec_batch, dec_seq, _ = x_sample.shape
    past_len = page_table.shape[1] * PAGE_SIZE
    alpha = (2 * depth) ** 0.25
    tm_p, tq, tm_s, x_groups_s, cpg = 512, KT_TILE, 256, 8, 32

    tables_p = _prompt_tables(jnp.arange(seq, dtype=jnp.float32))
    cos_s, sin_s = _rope_cos_sin(past_len + jnp.arange(dec_seq, dtype=jnp.float32))
    cos_s = jnp.tile(cos_s, (tm_s // dec_seq, MLA_HEADS))
    sin_s = jnp.tile(sin_s, (tm_s // dec_seq, MLA_HEADS))

    hp = x_prompt.reshape(batch * seq, D_MODEL)
    hs = x_sample.reshape(dec_batch * dec_seq, D_MODEL)
    mem_shape = (MEM_LEN, X_HEADS, X_HEAD_DIM)
    outs = [[] for _ in range(8)]
    for l in range(depth):
        w = _pack_weights(w_in[l], q_norm_g[l], kv_norm_g[l], w_uq[l], w_uk[l], w_uv[l], conv_w[l],
                          w_mk[l], w_mv[l], w_p_mla[l], w_p_conv[l], w_p_x[l], w_o[l], ln_g[l], ln_b[l])
        qp, kt, v, ckv_p, kr_p = _qkv_prompt(hp, w, tables_p, tm_p, seq // tm_p)
        ymla = _mla_prompt(qp, kt, v, batch, seq, tq)
        memk, memv = _memkv(mem_prompt.reshape(batch * MEM_LEN, D_MODEL), w["wmk"], w["wmv"], MEM_LEN)
        mem_kv = (memk.reshape(batch, MEM_LEN, X_WIDTH), memv.reshape(batch, MEM_LEN, X_WIDTH))
        hp, pre_p = _merge(hp, ymla, mem_kv, w, tm_p, seq, seq // tm_p, alpha)
        memk = memk.reshape(batch, *mem_shape)
        memv = memv.reshape(batch, *mem_shape)
        qlat, qrope, ckv_s, kr_s = _qkv_sample(hs, w, cos_s, sin_s, tm_s)
        o_lat = _mla_decode(page_table, qlat, qrope, ckv_s, kr_s, cache_ckv[l],
                            jnp.swapaxes(cache_kr[l], 1, 2), dec_seq, cpg)
        ymla = _uv(o_lat, w["wuv"], tm_s)
        interleaved = (dec_batch, MEM_LEN * X_HEADS, X_HEAD_DIM)
        yx = _xattn(hs, w["wqx"], cache_mem_k[l].reshape(interleaved),
                    cache_mem_v[l].reshape(interleaved), x_groups_s, dec_seq)
        hs, pre_s = _merge(hs, ymla, (yx, cache_conv[l]), w, tm_s, dec_seq, None, alpha)

        keep = CONV_WIDTH - 1
        for lst, val in zip(outs, (
                ckv_p.reshape(batch, seq, KV_LORA), jnp.swapaxes(kr_p, 1, 2),
                pre_p.reshape(batch, 8, CONV_DIM)[:, 8 - keep:],
                memk, memv,
                ckv_s.reshape(dec_batch, dec_seq, KV_LORA), kr_s.reshape(dec_batch, dec_seq, ROPE_DIM),
                pre_s.reshape(dec_batch, dec_seq, CONV_DIM)[:, dec_seq - keep:])):
            lst.append(val)
    return (hp.reshape(batch, seq, D_MODEL), hs.reshape(dec_batch, dec_seq, D_MODEL),
            *[jnp.stack(lst) for lst in outs])
```

```python
import functools
import math

import jax
import jax.numpy as jnp
import numpy as np
from jax import lax
from jax.experimental import pallas as pl
from jax.experimental.pallas import tpu as pltpu

D_MODEL = 1024
PAGE_SIZE = 128
MLA_HEADS = 8
NOPE_DIM = 64
ROPE_DIM = 32
V_DIM = 64
Q_LORA = 384
KV_LORA = 256
MLA_WIDTH = MLA_HEADS * V_DIM
MLA_SCALE = (NOPE_DIM + ROPE_DIM) ** -0.5
ROPE_THETA = 10000.0
CONV_DIM = D_MODEL // 2
CONV_WIDTH = 3
MEM_LEN = 256
X_HEADS = 4
X_HEAD_DIM = 128
X_WIDTH = X_HEADS * X_HEAD_DIM
X_SCALE = X_HEAD_DIM ** -0.5
N_BRANCH = 3
NORM_EPS = 1e-6
NEG_INF = -1e30
SPLIT_SIZES = (Q_LORA, KV_LORA, ROPE_DIM, MLA_WIDTH,
               CONV_DIM, CONV_DIM, CONV_DIM, CONV_DIM,
               X_WIDTH, X_WIDTH, N_BRANCH * D_MODEL)
SPLIT_POINTS = tuple(int(v) for v in np.cumsum(SPLIT_SIZES)[:-1])

LANES = 128
ROPE_LANES = MLA_HEADS * ROPE_DIM
HEAD_LANES = LANES
PACKED = MLA_HEADS * HEAD_LANES
KT_TILE = 256
FETCH_AHEAD = 2
N_PAGE_SLOTS = FETCH_AHEAD + 2
VMEM_LIMIT = 48 * 1024 * 1024
BF16 = jnp.bfloat16
F32 = jnp.float32
NT_DIMS = (((1,), (1,)), ((), ()))
MLA_EXP2_SCALE = MLA_SCALE * math.log2(math.e)


def _dot(a, b):
    return jnp.dot(a, b, preferred_element_type=F32)


def _dot_nt(a, b):
    return lax.dot_general(a, b, NT_DIMS, preferred_element_type=F32)


def _rms(x, g):
    return x * lax.rsqrt(jnp.mean(x * x, axis=-1, keepdims=True) + NORM_EPS) * g


def _const_spec(shape):
    return pl.BlockSpec(shape, lambda *_: (0,) * len(shape), pipeline_mode=pl.Buffered(1))


def _mod_pow2(x, n):
    assert n & (n - 1) == 0
    return x & (n - 1)


def _params(n_axes):
    return pltpu.CompilerParams(dimension_semantics=("arbitrary",) * n_axes,
                                vmem_limit_bytes=VMEM_LIMIT)


def _qkv_prompt_kernel(x_ref, wa_ref, gq_ref, gkv_ref, wuq_ref, wukt_ref, wuv_ref,
                       cq_ref, sq_ref, ck_ref, sk_ref,
                       qp_ref, kt_ref, v_ref, ckv_ref, kr_ref):
    xb = x_ref[...].astype(BF16)
    pa = _dot(xb, wa_ref[...])
    cqn = _rms(pa[:, :Q_LORA], gq_ref[...])
    qa = _dot(cqn.astype(BF16), wuq_ref[...])
    cq = cq_ref[...]
    sq = sq_ref[...]
    for h in range(MLA_HEADS):
        lo = h * HEAD_LANES
        qp_ref[h] = (qa[:, lo:lo + HEAD_LANES] * cq
                     + qa[:, PACKED + lo:PACKED + lo + HEAD_LANES] * sq).astype(BF16)
    ckvn = _rms(pa[:, Q_LORA:Q_LORA + KV_LORA], gkv_ref[...])
    ckv_ref[...] = ckvn
    vp = _dot(ckvn.astype(BF16), wuv_ref[...])
    for pair in range(MLA_HEADS // 2):
        v_ref[pair] = vp[:, pair * HEAD_LANES:(pair + 1) * HEAD_LANES].astype(BF16)
    k0 = Q_LORA + KV_LORA
    krot = pa[:, k0:k0 + HEAD_LANES] * ck_ref[...] + pa[:, k0 + HEAD_LANES:] * sk_ref[...]
    lane = lax.broadcasted_iota(jnp.int32, krot.shape, 1)
    k_rope = jnp.where(lane >= NOPE_DIM, krot, 0.0)
    for t in range(kt_ref.shape[0]):
        rows = slice(t * KT_TILE, (t + 1) * KT_TILE)
        k_nope_t = _dot(wukt_ref[...], ckvn[rows, :].T.astype(BF16))
        k_rope_t = k_rope[rows, :].T
        kr_ref[0, :, rows] = k_rope_t[NOPE_DIM:NOPE_DIM + ROPE_DIM, :]
        for h in range(MLA_HEADS):
            kt_ref[t, h] = (k_nope_t[h * HEAD_LANES:(h + 1) * HEAD_LANES, :] + k_rope_t).astype(BF16)


def _qkv_prompt(x2d, w, tables, tm, table_tiles):
    ntok = x2d.shape[0]
    tok = lambda width: pl.BlockSpec((tm, width), lambda t: (t, 0))
    tab = pl.BlockSpec((tm, HEAD_LANES), lambda t: (t % table_tiles, 0))
    head_tok = pl.BlockSpec((MLA_HEADS, tm, HEAD_LANES), lambda t: (0, t, 0))
    weights = [w["wa_p"], w["gq"], w["gkv"], w["wuq_p"], w["wukt_p"], w["wuv_p"]]
    return pl.pallas_call(
        _qkv_prompt_kernel,
        grid=(ntok // tm,),
        in_specs=[tok(D_MODEL)] + [_const_spec(a.shape) for a in weights] + [tab] * 4,
        out_specs=[head_tok,
                   pl.BlockSpec((tm // KT_TILE, MLA_HEADS, HEAD_LANES, KT_TILE), lambda t: (t, 0, 0, 0)),
                   pl.BlockSpec((MLA_HEADS // 2, tm, HEAD_LANES), lambda t: (0, t, 0)),
                   tok(KV_LORA),
                   pl.BlockSpec((1, ROPE_DIM, tm), lambda t: (t // table_tiles, 0, t % table_tiles))],
        out_shape=[jax.ShapeDtypeStruct((MLA_HEADS, ntok, HEAD_LANES), BF16),
                   jax.ShapeDtypeStruct((ntok // KT_TILE, MLA_HEADS, HEAD_LANES, KT_TILE), BF16),
                   jax.ShapeDtypeStruct((MLA_HEADS // 2, ntok, HEAD_LANES), BF16),
                   jax.ShapeDtypeStruct((ntok, KV_LORA), F32),
                   jax.ShapeDtypeStruct((ntok // (tm * table_tiles), ROPE_DIM, tm * table_tiles), F32)],
        compiler_params=_params(1),
        name="qkv_prompt",
    )(x2d, *weights, *tables)


def _qkv_sample_kernel(x_ref, wa_ref, gq_ref, gkv_ref, wuq_ref, wuk_ref, cos_ref, sin_ref,
                       qlat_ref, qrope_ref, ckv_ref, kr_ref):
    cos = cos_ref[...]
    sin = sin_ref[...]
    xb = x_ref[...].astype(BF16)
    pa = _dot(xb, wa_ref[...])
    cqn = _rms(pa[:, :Q_LORA], gq_ref[...])
    qa = _dot(cqn.astype(BF16), wuq_ref[...])
    r0 = MLA_HEADS * NOPE_DIM
    qrope_ref[...] = qa[:, r0:r0 + ROPE_LANES] * cos + qa[:, r0 + ROPE_LANES:] * sin
    for p in range(MLA_HEADS // 2):
        ql = _dot(qa[:, 128 * p:128 * (p + 1)].astype(BF16), wuk_ref[p])
        qlat_ref[2 * p] = ql[:, :KV_LORA]
        qlat_ref[2 * p + 1] = ql[:, KV_LORA:]
    ckv_ref[...] = _rms(pa[:, Q_LORA:Q_LORA + KV_LORA], gkv_ref[...])
    k0 = Q_LORA + KV_LORA
    krt = pa[:, k0:k0 + ROPE_LANES] * cos + pa[:, k0 + ROPE_LANES:] * sin
    kr_ref[...] = krt[:, :ROPE_DIM]


def _qkv_sample(x2d, w, cos, sin, tm):
    ntok = x2d.shape[0]
    tok = lambda width: pl.BlockSpec((tm, width), lambda t: (t, 0))
    tab = pl.BlockSpec((tm, ROPE_LANES), lambda t: (0, 0))
    weights = [w["wa"], w["gq"], w["gkv"], w["wuq"], w["wuk"]]
    return pl.pallas_call(
        _qkv_sample_kernel,
        grid=(ntok // tm,),
        in_specs=[tok(D_MODEL)] + [_const_spec(a.shape) for a in weights] + [tab, tab],
        out_specs=[pl.BlockSpec((MLA_HEADS, tm, KV_LORA), lambda t: (0, t, 0)),
                   tok(ROPE_LANES), tok(KV_LORA), tok(ROPE_DIM)],
        out_shape=[jax.ShapeDtypeStruct((MLA_HEADS, ntok, KV_LORA), F32),
                   jax.ShapeDtypeStruct((ntok, ROPE_LANES), F32),
                   jax.ShapeDtypeStruct((ntok, KV_LORA), F32),
                   jax.ShapeDtypeStruct((ntok, ROPE_DIM), F32)],
        compiler_params=_params(1),
        name="qkv_sample",
    )(x2d, *weights, cos, sin)


def _lane_repeat(x, width):
    return jnp.concatenate([x] * (width // LANES), axis=-1)


def _softmax_weights(s, m_old, l_old, scale):
    mul = (lambda x: x) if scale == 1.0 else (lambda x: x * scale)
    m_new = jnp.maximum(m_old, jnp.max(s, axis=-1, keepdims=True))
    alpha = jnp.exp2(mul(m_old - m_new))
    p = jnp.exp2(mul(s - _lane_repeat(m_new, s.shape[-1])))
    l_new = alpha * l_old + jnp.sum(p, axis=-1, keepdims=True)
    return m_new, l_new, alpha, p.astype(BF16)


def _softmax_init(m_scr, l_scr, acc_scr):
    m_scr[...] = jnp.full(m_scr.shape, -jnp.inf, F32)
    l_scr[...] = jnp.zeros(l_scr.shape, F32)
    acc_scr[...] = jnp.zeros(acc_scr.shape, F32)


def _mla_prompt_kernel(qp_ref, kt_ref, v_ref, y_ref, m_scr, l_scr, acc_scr, *, tq, rq):
    assert tq == KT_TILE and tq % rq == 0
    i = pl.program_id(1)
    _softmax_init(m_scr, l_scr, acc_scr)
    chunks = [(h, r0) for h in range(MLA_HEADS) for r0 in range(0, tq, rq)]
    first_of_pair = lax.broadcasted_iota(jnp.int32, (rq, HEAD_LANES), 1) < V_DIM

    def score(j, n_tiles):
        return [jnp.concatenate([_dot(qp_ref[h, r0:r0 + rq, :], kt_ref[j + t, h])
                                 for t in range(n_tiles)], axis=1) for h, r0 in chunks]

    def softmax(scores, diagonal):
        weights = {}
        for (h, r0), s in zip(chunks, scores):
            rs = slice(r0, r0 + rq)
            if diagonal:
                q_pos = r0 + lax.broadcasted_iota(jnp.int32, (rq, tq), 0)
                k_pos = lax.broadcasted_iota(jnp.int32, (rq, tq), 1)
                s = jnp.where(k_pos <= q_pos, s, NEG_INF)
            m_scr[h, rs, :], l_scr[h, rs, :], alpha, p = _softmax_weights(
                s, m_scr[h, rs, :], l_scr[h, rs, :], 1.0)
            weights[h, r0] = (alpha, p)
        return weights

    def weigh(j, n_tiles, weights):
        keys = pl.ds(pl.multiple_of(j * tq, tq), n_tiles * tq)
        for pair in range(MLA_HEADS // 2):
            ha, hb = 2 * pair, 2 * pair + 1
            for r0 in range(0, tq, rq):
                rs = slice(r0, r0 + rq)
                (alpha_a, p_a), (alpha_b, p_b) = weights[ha, r0], weights[hb, r0]
                v_pair = v_ref[pair, keys, :]
                pv = jnp.where(first_of_pair, _dot(p_a, v_pair), _dot(p_b, v_pair))
                acc_scr[pair, rs, :] = (jnp.where(first_of_pair, alpha_a, alpha_b)
                                        * acc_scr[pair, rs, :] + pv)

    def step(j, n_tiles, diagonal):
        weigh(j, n_tiles, softmax(score(j, n_tiles), diagonal))

    def body(j4, carry):
        step(4 * j4, 4, False)
        return carry

    n_quads = lax.shift_right_logical(i, 2)
    lax.fori_loop(0, n_quads, body, 0)

    @pl.when((i & 2) != 0)
    def _():
        step(4 * n_quads, 2, False)

    @pl.when((i & 1) != 0)
    def _():
        step(i - 1, 1, False)

    step(i, 1, True)
    first = lax.broadcasted_iota(jnp.int32, (tq, HEAD_LANES), 1) < V_DIM
    for pair in range(MLA_HEADS // 2):
        inv_l = jnp.where(first, 1.0 / l_scr[2 * pair], 1.0 / l_scr[2 * pair + 1])
        y_ref[:, pair * HEAD_LANES:(pair + 1) * HEAD_LANES] = acc_scr[pair] * inv_l


def _mla_prompt(qp, kt, v, batch, seq, tq):
    ntok = batch * seq
    nq = seq // tq
    return pl.pallas_call(
        functools.partial(_mla_prompt_kernel, tq=tq, rq=256),
        grid=(batch, nq),
        in_specs=[pl.BlockSpec((MLA_HEADS, tq, HEAD_LANES), lambda b, i: (0, b * nq + i, 0)),
                  pl.BlockSpec((seq // KT_TILE, MLA_HEADS, HEAD_LANES, KT_TILE),
                               lambda b, i: (b, 0, 0, 0)),
                  pl.BlockSpec((MLA_HEADS // 2, seq, HEAD_LANES), lambda b, i: (0, b, 0))],
        out_specs=pl.BlockSpec((tq, MLA_WIDTH), lambda b, i: (b * nq + i, 0)),
        out_shape=jax.ShapeDtypeStruct((ntok, MLA_WIDTH), F32),
        scratch_shapes=[pltpu.VMEM((MLA_HEADS, tq, LANES), F32),
                        pltpu.VMEM((MLA_HEADS, tq, LANES), F32),
                        pltpu.VMEM((MLA_HEADS // 2, tq, HEAD_LANES), F32)],
        compiler_params=_params(2),
        name="mla_prompt",
    )(qp, kt, v)


def _mla_decode_kernel(pt_ref, qlat_ref, qrope_ref, ckvn_ref, krn_ref, ckv_hbm, krt_hbm, o_ref,
                       ckv_buf, krt_buf, sem, ql_scr, qr_scr, kn_scr, krn_scr, s_scr,
                       m_scr, l_scr, acc_scr, *, cpg, n_chunks, dec_seq):
    assert n_chunks % N_PAGE_SLOTS == 0 and N_PAGE_SLOTS == FETCH_AHEAD + 2
    seq = pl.program_id(0)
    n_seq = pl.num_programs(0)
    rows = MLA_HEADS * dec_seq

    def slot_of(c):
        return _mod_pow2(c, N_PAGE_SLOTS)

    def page_copies(sq, chunk, slot):
        copies = []
        for k in range(cpg):
            page = pt_ref[sq, chunk * cpg + k]
            copies.append(pltpu.make_async_copy(ckv_hbm.at[page], ckv_buf.at[slot, k], sem.at[slot, 0]))
            lanes = pl.ds(k * PAGE_SIZE, PAGE_SIZE)
            copies.append(pltpu.make_async_copy(krt_hbm.at[page], krt_buf.at[slot, :, lanes], sem.at[slot, 1]))
        return copies

    def start_chunk(sq, chunk, slot):
        for n, cp in enumerate(page_copies(sq, chunk, slot)):
            cp.start(priority=(n // 2) % 2)

    def chunk_ahead(c):
        ahead = c + FETCH_AHEAD
        spills = ahead >= n_chunks
        more_seqs = seq + 1 < n_seq
        sq = jnp.where(spills & more_seqs, seq + 1, seq)
        ch = jnp.where(spills, jnp.where(more_seqs, ahead - n_chunks, c), ahead)
        return sq, ch

    def fetch_ahead_and_wait(c):
        start_chunk(*chunk_ahead(c), slot_of(c + FETCH_AHEAD))
        for cp in page_copies(seq, c, slot_of(c)):
            cp.wait()

    def score(slot):
        ql = ql_scr[...]
        rope = _dot(qr_scr[...], krt_buf[slot].astype(BF16))
        for k in range(cpg):
            keys = slice(k * PAGE_SIZE, (k + 1) * PAGE_SIZE)
            s_scr[:, keys] = _dot_nt(ql, ckv_buf[slot, k].astype(BF16)) + rope[:, keys]

    chunk_keys = cpg * PAGE_SIZE

    def softmax():
        m_scr[...], l_scr[...], alpha, p = _softmax_weights(
            s_scr[:, :chunk_keys], m_scr[...], l_scr[...], MLA_EXP2_SCALE)
        return alpha, p

    def weighted(slot, p):
        pv = _dot(p[:, :PAGE_SIZE], ckv_buf[slot, 0].astype(BF16))
        for k in range(1, cpg):
            pv = pv + _dot(p[:, PAGE_SIZE * k:PAGE_SIZE * (k + 1)], ckv_buf[slot, k].astype(BF16))
        return pv

    def weigh(slot, alpha, p):
        acc_scr[...] = _lane_repeat(alpha, KV_LORA) * acc_scr[...] + weighted(slot, p)

    @pl.when(seq == 0)
    def _():
        for c in range(FETCH_AHEAD):
            start_chunk(0, c, c)

    ql_scr[...] = jnp.concatenate([qlat_ref[h] for h in range(MLA_HEADS)], axis=0).astype(BF16)
    qr = qrope_ref[...]
    qr_scr[...] = jnp.concatenate(
        [qr[:, ROPE_DIM * h:ROPE_DIM * (h + 1)] for h in range(MLA_HEADS)], axis=0).astype(BF16)
    kn_scr[...] = jnp.zeros(kn_scr.shape, F32)
    krn_scr[...] = jnp.zeros(krn_scr.shape, F32)
    kn_scr[:dec_seq, :] = ckvn_ref[...]
    krn_scr[:dec_seq, :] = krn_ref[...]
    _softmax_init(m_scr, l_scr, acc_scr)
    ck_new = kn_scr[...].astype(BF16)
    s_new = _dot_nt(ql_scr[...], ck_new) + _dot_nt(qr_scr[...], krn_scr[...].astype(BF16))
    q_pos = _mod_pow2(lax.broadcasted_iota(jnp.int32, (rows, PAGE_SIZE), 0), dec_seq)
    k_pos = lax.broadcasted_iota(jnp.int32, (rows, PAGE_SIZE), 1)
    s_scr[:, chunk_keys:] = jnp.where(k_pos <= q_pos, s_new, NEG_INF)

    fetch_ahead_and_wait(0)
    score(0)

    def chunk_body(c, carry):
        fetch_ahead_and_wait(c)
        alpha, p = softmax()
        score(slot_of(c))
        weigh(slot_of(c - 1), alpha, p)
        return carry

    lax.fori_loop(1, n_chunks, chunk_body, 0)
    _, l_fin, alpha, p = _softmax_weights(s_scr[...], m_scr[...], l_scr[...], MLA_EXP2_SCALE)
    pv = weighted(slot_of(n_chunks - 1), p) + _dot(p[:, chunk_keys:], ck_new)
    o = (_lane_repeat(alpha, KV_LORA) * acc_scr[...] + pv) * _lane_repeat(1.0 / l_fin, KV_LORA)

    @pl.when(seq == n_seq - 1)
    def _():
        for c in range(n_chunks - FETCH_AHEAD, n_chunks):
            for cp in page_copies(seq, c, slot_of(c + FETCH_AHEAD)):
                cp.wait()

    for h in range(MLA_HEADS):
        o_ref[h] = o[h * dec_seq:(h + 1) * dec_seq, :]


def _mla_decode(page_table, qlat, qrope, ckv_new, kr_new, cache_ckv, cache_krt, dec_seq, cpg):
    n_seq, n_pages = page_table.shape
    ntok = n_seq * dec_seq
    rows = MLA_HEADS * dec_seq
    tok = lambda width: pl.BlockSpec((dec_seq, width), lambda s, pt: (s, 0))
    head_tok = pl.BlockSpec((MLA_HEADS, dec_seq, KV_LORA), lambda s, pt: (0, s, 0))
    hbm = pl.BlockSpec(memory_space=pl.ANY)
    grid_spec = pltpu.PrefetchScalarGridSpec(
        num_scalar_prefetch=1,
        grid=(n_seq,),
        in_specs=[head_tok, tok(ROPE_LANES), tok(KV_LORA), tok(ROPE_DIM), hbm, hbm],
        out_specs=head_tok,
        scratch_shapes=[pltpu.VMEM((N_PAGE_SLOTS, cpg, PAGE_SIZE, KV_LORA), F32),
                        pltpu.VMEM((N_PAGE_SLOTS, ROPE_DIM, cpg * PAGE_SIZE), F32),
                        pltpu.SemaphoreType.DMA((N_PAGE_SLOTS, 2)),
                        pltpu.VMEM((rows, KV_LORA), BF16), pltpu.VMEM((rows, ROPE_DIM), BF16),
                        pltpu.VMEM((PAGE_SIZE, KV_LORA), F32), pltpu.VMEM((PAGE_SIZE, ROPE_DIM), F32),
                        pltpu.VMEM((rows, (cpg + 1) * PAGE_SIZE), F32),
                        pltpu.VMEM((rows, LANES), F32), pltpu.VMEM((rows, LANES), F32),
                        pltpu.VMEM((rows, KV_LORA), F32)],
    )
    return pl.pallas_call(
        functools.partial(_mla_decode_kernel, cpg=cpg, n_chunks=n_pages // cpg, dec_seq=dec_seq),
        grid_spec=grid_spec,
        out_shape=jax.ShapeDtypeStruct((MLA_HEADS, ntok, KV_LORA), F32),
        compiler_params=_params(1),
        name="mla_decode",
    )(page_table, qlat, qrope, ckv_new, kr_new, cache_ckv, cache_krt)


def _uv_kernel(o_ref, wuv_ref, y_ref):
    y = _dot(o_ref[0].astype(BF16), wuv_ref[0])
    for h in range(1, MLA_HEADS):
        y = y + _dot(o_ref[h].astype(BF16), wuv_ref[h])
    y_ref[...] = y


def _uv(o_lat, wuv, tm):
    ntok = o_lat.shape[1]
    return pl.pallas_call(
        _uv_kernel,
        grid=(ntok // tm,),
        in_specs=[pl.BlockSpec((MLA_HEADS, tm, KV_LORA), lambda t: (0, t, 0)), _const_spec(wuv.shape)],
        out_specs=pl.BlockSpec((tm, MLA_WIDTH), lambda t: (t, 0)),
        out_shape=jax.ShapeDtypeStruct((ntok, MLA_WIDTH), F32),
        compiler_params=_params(1),
        name="uv",
    )(o_lat, wuv)


def _memkv_kernel(mem_ref, wk_ref, wv_ref, k_ref, v_ref):
    mb = mem_ref[...].astype(BF16)
    k_ref[...] = _dot(mb, wk_ref[...])
    v_ref[...] = _dot(mb, wv_ref[...])


def _memkv(mem2d, wk, wv, tm):
    n = mem2d.shape[0]
    out = pl.BlockSpec((tm, X_WIDTH), lambda t: (t, 0))
    return pl.pallas_call(
        _memkv_kernel,
        grid=(n // tm,),
        in_specs=[pl.BlockSpec((tm, D_MODEL), lambda t: (t, 0)), _const_spec(wk.shape),
                  _const_spec(wv.shape)],
        out_specs=[out, out],
        out_shape=[jax.ShapeDtypeStruct((n, X_WIDTH), F32)] * 2,
        compiler_params=_params(1),
        name="memkv",
    )(mem2d, wk, wv)


def _xattn_kernel(x_ref, wqx_ref, mk_ref, mv_ref, y_ref, *, groups, g):
    qx = _dot(x_ref[...].astype(BF16), wqx_ref[...])
    units = [(s, h) for s in range(groups) for h in range(X_HEADS)]
    rows = lambda s: slice(s * g, (s + 1) * g)
    cols = lambda h: slice(h * X_HEAD_DIM, (h + 1) * X_HEAD_DIM)

    def head_of(ref, s, h):
        return ref[s, pl.ds(h, MEM_LEN, stride=X_HEADS), :].astype(BF16)

    scores = [_dot_nt(qx[rows(s), cols(h)].astype(BF16), head_of(mk_ref, s, h)) for s, h in units]
    weights = []
    for sc in scores:
        sc = sc * X_SCALE
        p = jnp.exp(sc - jnp.max(sc, axis=-1, keepdims=True))
        weights.append((p / jnp.sum(p, axis=-1, keepdims=True)).astype(BF16))
    for (s, h), p in zip(units, weights):
        y_ref[rows(s), cols(h)] = _dot(p, head_of(mv_ref, s, h))


def _xattn(x2d, wqx, mem_k, mem_v, groups, g):
    ntok = x2d.shape[0]
    tm = groups * g
    mem = pl.BlockSpec((groups, MEM_LEN * X_HEADS, X_HEAD_DIM), lambda t: (t, 0, 0))
    return pl.pallas_call(
        functools.partial(_xattn_kernel, groups=groups, g=g),
        grid=(ntok // tm,),
        in_specs=[pl.BlockSpec((tm, D_MODEL), lambda t: (t, 0)), _const_spec(wqx.shape), mem, mem],
        out_specs=pl.BlockSpec((tm, X_WIDTH), lambda t: (t, 0)),
        out_shape=jax.ShapeDtypeStruct((ntok, X_WIDTH), F32),
        compiler_params=_params(1),
        name="xattn",
    )(x2d, wqx, mem_k, mem_v)


def _merge_kernel(x_ref, ymla_ref, *refs, tm, seq_rows, tiles_per_seq, alpha):
    one_seq = tiles_per_seq is not None
    if one_seq:
        mk_ref, mv_ref, wqx_ref = refs[:3]
    else:
        yx_ref, prev_ref = refs[:2]
    (wb_ref, wpm_ref, wpc_ref, wpx_ref, wo_ref, convw_ref, lng_ref, lnb_ref,
     out_ref, pre_ref, carry_scr) = refs[3 if one_seq else 2:]
    x = x_ref[...]
    xb = x.astype(BF16)
    heads = [slice(h * X_HEAD_DIM, (h + 1) * X_HEAD_DIM) for h in range(X_HEADS)]
    if one_seq:
        qx = _dot(xb, wqx_ref[...])
        x_scores = [_dot_nt(qx[:, hs].astype(BF16), mk_ref[0, :, hs].astype(BF16)) for hs in heads]

    def proj(k):
        lo = k * CONV_DIM if k < 6 else 6 * CONV_DIM + (k - 6) * D_MODEL
        hi = lo + (CONV_DIM if k < 6 else D_MODEL)
        return _dot(xb, wb_ref[:, lo:hi])

    def branch(y, z, gate, wp_ref):
        return jax.nn.sigmoid(gate) * _dot((y * jax.nn.silu(z)).astype(BF16), wp_ref[...])

    m = branch(ymla_ref[...], proj(0), proj(6), wpm_ref)

    pre = proj(2) * proj(3)
    row = lax.broadcasted_iota(jnp.int32, (tm, CONV_DIM), 0)
    if one_seq:
        @pl.when(pl.program_id(0) % tiles_per_seq == 0)
        def _():
            carry_scr[...] = jnp.zeros(carry_scr.shape, F32)
        prev1 = jnp.broadcast_to(carry_scr[7:8, :], (tm, CONV_DIM))
        prev0 = jnp.broadcast_to(carry_scr[6:7, :], (tm, CONV_DIM))
        pos = row
    else:
        n_seq = tm // seq_rows
        prev = prev_ref[...]
        bc = lambda r: jnp.broadcast_to(prev[:, r:r + 1, :], (n_seq, seq_rows, CONV_DIM)).reshape(tm, CONV_DIM)
        prev0, prev1 = bc(0), bc(1)
        pos = _mod_pow2(row, seq_rows)
    back1 = jnp.where(pos == 0, prev1, pltpu.roll(pre, 1, 0))
    back2 = jnp.where(pos == 0, prev0, jnp.where(pos == 1, prev1, pltpu.roll(pre, 2, 0)))
    cw = convw_ref[...]
    conv = cw[0:1, :] * back2 + cw[1:2, :] * back1 + cw[2:3, :] * pre
    if one_seq:
        pre_ref[...] = pre[tm - 8:, :]
        carry_scr[...] = pre[tm - 8:, :]
    else:
        pre_ref[...] = pre
    m = m + branch(proj(1) * conv, proj(4), proj(7), wpc_ref)

    if one_seq:
        outs = []
        for hs, sc in zip(heads, x_scores):
            sc = sc * X_SCALE
            p = jnp.exp(sc - jnp.max(sc, axis=-1, keepdims=True))
            p = (p / jnp.sum(p, axis=-1, keepdims=True)).astype(BF16)
            outs.append(_dot(p, mv_ref[0, :, hs].astype(BF16)))
        yx = jnp.concatenate(outs, axis=-1)
    else:
        yx = yx_ref[...]
    m = m + branch(yx, proj(5), proj(8), wpx_ref)

    r = alpha * x + _dot(m.astype(BF16), wo_ref[...])
    mu = jnp.mean(r, axis=-1, keepdims=True)
    d = r - mu
    var = jnp.mean(d * d, axis=-1, keepdims=True)
    out_ref[...] = d * lax.rsqrt(var + NORM_EPS) * lng_ref[...] + lnb_ref[...]


def _merge(x2d, ymla, extra, w, tm, seq_rows, tiles_per_seq, alpha):
    ntok = x2d.shape[0]
    tok = lambda width: pl.BlockSpec((tm, width), lambda t: (t, 0))
    weights = [w["wb"], w["wpm"], w["wpc"], w["wpx"], w["wo"], w["convw"], w["lng"], w["lnb"]]
    if tiles_per_seq is None:
        pre_spec, pre_rows = tok(CONV_DIM), ntok
        extra_specs = [tok(X_WIDTH),
                       pl.BlockSpec((tm // seq_rows, CONV_WIDTH - 1, CONV_DIM), lambda t: (t, 0, 0))]
    else:
        pre_spec = pl.BlockSpec((8, CONV_DIM), lambda t: (t // tiles_per_seq, 0))
        pre_rows = 8 * (ntok // tm // tiles_per_seq)
        mem = pl.BlockSpec((1, MEM_LEN, X_WIDTH), lambda t: (t // tiles_per_seq, 0, 0))
        extra = (*extra, w["wqx"])
        extra_specs = [mem, mem, _const_spec(w["wqx"].shape)]
    return pl.pallas_call(
        functools.partial(_merge_kernel, tm=tm, seq_rows=seq_rows, tiles_per_seq=tiles_per_seq,
                          alpha=alpha),
        grid=(ntok // tm,),
        in_specs=[tok(D_MODEL), tok(MLA_WIDTH)] + extra_specs
        + [_const_spec(a.shape) for a in weights],
        out_specs=[tok(D_MODEL), pre_spec],
        out_shape=[jax.ShapeDtypeStruct((ntok, D_MODEL), F32),
                   jax.ShapeDtypeStruct((pre_rows, CONV_DIM), F32)],
        scratch_shapes=[pltpu.VMEM((8, CONV_DIM), F32)],
        compiler_params=_params(1),
        name="merge",
    )(x2d, ymla, *extra, *weights)


def _swap_halves(w):
    half = ROPE_DIM // 2
    return jnp.concatenate([w[..., half:], w[..., :half]], axis=-1)


def _place_rope(w):
    z = jnp.zeros_like(w)
    return jnp.concatenate([w, z, w, z], axis=-1)


def _pack_weights(w_in, q_norm_g, kv_norm_g, w_uq, w_uk, w_uv, conv_w, w_mk, w_mv,
                  w_p_mla, w_p_conv, w_p_x, w_o, ln_g, ln_b):
    (w_cq, w_ckv, w_kr, w_zmla, w_cb, w_cc, w_ch, w_zc, w_qx, w_zx, w_g) = jnp.split(
        w_in, SPLIT_POINTS, axis=-1)
    tile_heads = lambda a: jnp.tile(a, (1, MLA_HEADS))
    uq = w_uq.reshape(Q_LORA, MLA_HEADS, NOPE_DIM + ROPE_DIM)
    uq_nope, uq_rope = uq[:, :, :NOPE_DIM], uq[:, :, NOPE_DIM:]
    uk_t = jnp.transpose(w_uk, (1, 2, 0))
    uv_t = jnp.transpose(w_uv, (1, 0, 2))
    wa = jnp.concatenate([w_cq, w_ckv, tile_heads(w_kr), tile_heads(_swap_halves(w_kr))], axis=1)
    wuq = jnp.concatenate([uq_nope.reshape(Q_LORA, -1), uq_rope.reshape(Q_LORA, -1),
                           _swap_halves(uq_rope).reshape(Q_LORA, -1)], axis=1)
    zeros = jnp.zeros((MLA_HEADS // 2, NOPE_DIM, KV_LORA), w_uk.dtype)
    wuk = jnp.concatenate([jnp.concatenate([uk_t[0::2], zeros], axis=2),
                           jnp.concatenate([zeros, uk_t[1::2]], axis=2)], axis=1)
    wuv = jnp.stack([jnp.pad(uv_t[h], ((0, 0), (h * V_DIM, MLA_WIDTH - (h + 1) * V_DIM)))
                     for h in range(MLA_HEADS)])
    wa_p = jnp.concatenate([w_cq, w_ckv, _place_rope(w_kr), _place_rope(_swap_halves(w_kr))], axis=1)
    pad_q = jnp.zeros((Q_LORA, MLA_HEADS, HEAD_LANES - NOPE_DIM - ROPE_DIM), w_uq.dtype)
    wuq_p = jnp.concatenate([
        jnp.concatenate([uq_nope, uq_rope, pad_q], axis=-1).reshape(Q_LORA, PACKED),
        jnp.concatenate([jnp.zeros_like(uq_nope), _swap_halves(uq_rope), pad_q],
                        axis=-1).reshape(Q_LORA, PACKED)], axis=1)
    wukt_p = jnp.concatenate([uk_t, jnp.zeros((MLA_HEADS, HEAD_LANES - NOPE_DIM, KV_LORA), w_uk.dtype)],
                             axis=1).reshape(PACKED, KV_LORA)
    wuv_p = w_uv.reshape(KV_LORA, MLA_WIDTH)
    wb = jnp.concatenate([w_zmla, w_cb, w_cc, w_ch, w_zc, w_zx, w_g], axis=1)
    bf = lambda a: a.astype(BF16)
    row = lambda a: a.reshape(1, -1).astype(F32)
    return dict(wa=bf(wa), gq=row(q_norm_g), gkv=row(kv_norm_g), wuq=bf(wuq), wuk=bf(wuk),
                wuv=bf(wuv), wa_p=bf(wa_p), wuq_p=bf(wuq_p), wukt_p=bf(wukt_p), wuv_p=bf(wuv_p),
                wqx=bf(w_qx), wmk=bf(w_mk), wmv=bf(w_mv), wb=bf(wb),
                wpm=bf(w_p_mla), wpc=bf(w_p_conv), wpx=bf(w_p_x), wo=bf(w_o),
                convw=conv_w.astype(F32), lng=row(ln_g), lnb=row(ln_b))


def _rope_cos_sin(pos):
    half = ROPE_DIM // 2
    inv_freq = ROPE_THETA ** (-jnp.arange(half, dtype=jnp.float32) * (2.0 / ROPE_DIM))
    ang = pos.astype(jnp.float32)[:, None] * inv_freq[None, :]
    cos, sin = jnp.cos(ang), jnp.sin(ang)
    return jnp.concatenate([cos, cos], axis=-1), jnp.concatenate([-sin, sin], axis=-1)


def _prompt_tables(pos):
    half = ROPE_DIM // 2
    inv_freq = ROPE_THETA ** (-jnp.arange(half, dtype=jnp.float32) * (2.0 / ROPE_DIM))
    lane = np.arange(HEAD_LANES)

    def cos_sin(rope_starts):
        dim = np.zeros(HEAD_LANES, np.int32)
        on = np.zeros(HEAD_LANES, bool)
        for start in rope_starts:
            dim[start:start + ROPE_DIM] = np.arange(ROPE_DIM)
            on[start:start + ROPE_DIM] = True
        freq = jnp.where(on, inv_freq[dim % half], 0.0)
        sign = np.where(dim < half, -1.0, 1.0).astype(np.float32)
        ang = pos.astype(jnp.float32)[:, None] * freq[None, :]
        return jnp.cos(ang), jnp.sin(ang) * sign, on

    cos_q, sin_q, _ = cos_sin([NOPE_DIM])
    cos_k, sin_k, on_k = cos_sin([0, NOPE_DIM])
    cq = jnp.where(lane < NOPE_DIM + ROPE_DIM, cos_q, 0.0) * MLA_EXP2_SCALE
    return cq, sin_q * MLA_EXP2_SCALE, jnp.where(on_k, cos_k, 0.0), sin_k


def kernel(x_prompt, x_sample, mem_prompt, cache_ckv, cache_kr, cache_conv, cache_mem_k,
           cache_mem_v, page_table, w_in, q_norm_g, kv_norm_g, w_uq, w_uk, w_uv, conv_w,
           w_mk, w_mv, w_p_mla, w_p_conv, w_p_x, w_o, ln_g, ln_b):
    depth = w_in.shape[0]
    batch, seq, _ = x_prompt.shape
    dec_batch, dec_seq, _ = x_sample.shape
    past_len = page_table.shape[1] * PAGE_SIZE
    alpha = (2 * depth) ** 0.25
    tm_p, tq, tm_s, x_groups_s, cpg = 512, KT_TILE, 256, 8, 32

    tables_p = _prompt_tables(jnp.arange(seq, dtype=jnp.float32))
    cos_s, sin_s = _rope_cos_sin(past_len + jnp.arange(dec_seq, dtype=jnp.float32))
    cos_s = jnp.tile(cos_s, (tm_s // dec_seq, MLA_HEADS))
    sin_s = jnp.tile(sin_s, (tm_s // dec_seq, MLA_HEADS))

    hp = x_prompt.reshape(batch * seq, D_MODEL)
    hs = x_sample.reshape(dec_batch * dec_seq, D_MODEL)
    mem_shape = (MEM_LEN, X_HEADS, X_HEAD_DIM)
    outs = [[] for _ in range(8)]
    for l in range(depth):
        w = _pack_weights(w_in[l], q_norm_g[l], kv_norm_g[l], w_uq[l], w_uk[l], w_uv[l], conv_w[l],
                          w_mk[l], w_mv[l], w_p_mla[l], w_p_conv[l], w_p_x[l], w_o[l], ln_g[l], ln_b[l])
        qp, kt, v, ckv_p, kr_p = _qkv_prompt(hp, w, tables_p, tm_p, seq // tm_p)
        ymla = _mla_prompt(qp, kt, v, batch, seq, tq)
        memk, memv = _memkv(mem_prompt.reshape(batch * MEM_LEN, D_MODEL), w["wmk"], w["wmv"], MEM_LEN)
        mem_kv = (memk.reshape(batch, MEM_LEN, X_WIDTH), memv.reshape(batch, MEM_LEN, X_WIDTH))
        hp, pre_p = _merge(hp, ymla, mem_kv, w, tm_p, seq, seq // tm_p, alpha)
        memk = memk.reshape(batch, *mem_shape)
        memv = memv.reshape(batch, *mem_shape)
        qlat, qrope, ckv_s, kr_s = _qkv_sample(hs, w, cos_s, sin_s, tm_s)
        o_lat = _mla_decode(page_table, qlat, qrope, ckv_s, kr_s, cache_ckv[l],
                            jnp.swapaxes(cache_kr[l], 1, 2), dec_seq, cpg)
        ymla = _uv(o_lat, w["wuv"], tm_s)
        interleaved = (dec_batch, MEM_LEN * X_HEADS, X_HEAD_DIM)
        yx = _xattn(hs, w["wqx"], cache_mem_k[l].reshape(interleaved),
                    cache_mem_v[l].reshape(interleaved), x_groups_s, dec_seq)
        hs, pre_s = _merge(hs, ymla, (yx, cache_conv[l]), w, tm_s, dec_seq, None, alpha)

        keep = CONV_WIDTH - 1
        for lst, val in zip(outs, (
                ckv_p.reshape(batch, seq, KV_LORA), jnp.swapaxes(kr_p, 1, 2),
                pre_p.reshape(batch, 8, CONV_DIM)[:, 8 - keep:],
                memk, memv,
                ckv_s.reshape(dec_batch, dec_seq, KV_LORA), kr_s.reshape(dec_batch, dec_seq, ROPE_DIM),
                pre_s.reshape(dec_batch, dec_seq, CONV_DIM)[:, dec_seq - keep:])):
            lst.append(val)
    return (hp.reshape(batch, seq, D_MODEL), hs.reshape(dec_batch, dec_seq, D_MODEL),
            *[jnp.stack(lst) for lst in outs])
```
